```python
import jax, jax.numpy as jnp
from jax import lax
import numpy as np

D_MODEL = 2048
BATCH = 4
SEQ = 2048
DEPTH = 1

D_MIX = D_MODEL
D_CONV = D_MIX // 2
D_MLSTM = D_MIX - D_CONV
N_MLSTM_HEADS = 4
DV_HEAD = D_MLSTM // N_MLSTM_HEADS
DQK_HEAD = DV_HEAD // 2
D_QK = N_MLSTM_HEADS * DQK_HEAD
CONV_WIDTH = 3
CHUNK = 64
D_FF = 5632
D_PLE = 256
EPS = 1e-6
D_IN_PROJ = 3 * D_CONV + 2 * D_QK + 2 * D_MLSTM + 2 * N_MLSTM_HEADS

kernel_name = "hymba_style_shortconv_mlstm_convffn_ple"


def rmsnorm(x, g):
    xf = x.astype(jnp.float32)
    y = xf * lax.rsqrt(jnp.mean(xf * xf, axis=-1, keepdims=True) + EPS)
    return (y * g.astype(jnp.float32)).astype(x.dtype)


def causal_dwconv3(u, w):
    S = u.shape[1]
    up = jnp.pad(u, ((0, 0), (CONV_WIDTH - 1, 0), (0, 0)))
    y = up[:, 0:S] * w[0]
    for j in range(1, CONV_WIDTH):
        y = y + up[:, j:j + S] * w[j]
    return y


def mlstm_chunkwise(q, k, v, ig, fg):
    Bn, H, S, dk = q.shape
    dv = v.shape[-1]
    nc = S // CHUNK

    def to_chunks(a):
        a = a.astype(jnp.float32).reshape((Bn, H, nc, CHUNK) + a.shape[3:])
        return jnp.moveaxis(a, 2, 0)

    logf = jax.nn.log_sigmoid(fg.astype(jnp.float32))
    logi = ig.astype(jnp.float32)
    xs = (to_chunks(q), to_chunks(k), to_chunks(v), to_chunks(logf), to_chunks(logi))
    causal = jnp.tril(jnp.ones((CHUNK, CHUNK), dtype=bool))

    def step(carry, xs_c):
        C, n, m = carry
        qb, kb, vb, lf, li = xs_c
        b = jnp.cumsum(lf, axis=-1)
        dlog = b[..., :, None] - b[..., None, :] + li[..., None, :]
        dlog = jnp.where(causal, dlog, -jnp.inf)
        inter = b + m[..., None]
        m_t = jnp.maximum(inter, jnp.max(dlog, axis=-1))
        dw = jnp.exp(dlog - m_t[..., None])
        a_inter = jnp.exp(inter - m_t)
        s = jnp.einsum('bhtd,bhsd->bhts', qb, kb) * dw
        num = (a_inter[..., None] * jnp.einsum('bhvd,bhtd->bhtv', C, qb)
               + jnp.einsum('bhts,bhsv->bhtv', s, vb))
        den = a_inter * jnp.einsum('bhd,bhtd->bht', n, qb) + jnp.sum(s, axis=-1)
        den = jnp.maximum(jnp.abs(den), jnp.exp(-m_t))
        h = num / den[..., None]
        b_tot = b[..., -1]
        wlog = b_tot[..., None] - b + li
        m_new = jnp.maximum(b_tot + m, jnp.max(wlog, axis=-1))
        a_prev = jnp.exp(b_tot + m - m_new)
        w = jnp.exp(wlog - m_new[..., None])
        C_new = a_prev[..., None, None] * C + jnp.einsum('bhs,bhsv,bhsd->bhvd', w, vb, kb)
        n_new = a_prev[..., None] * n + jnp.einsum('bhs,bhsd->bhd', w, kb)
        return (C_new, n_new, m_new), h

    init = (jnp.zeros((Bn, H, dv, dk), jnp.float32),
            jnp.zeros((Bn, H, dk), jnp.float32),
            jnp.zeros((Bn, H), jnp.float32))
    _, hs = lax.scan(step, init, xs)
    return jnp.moveaxis(hs, 0, 2).reshape(Bn, H, S, dv)


def hybrid_mixer(h, w_in, b_igate, b_fgate, short_conv_w, mh_norm_g, w_out):
    Bn, S, _ = h.shape
    proj = h @ w_in
    idx = np.cumsum([D_CONV, D_CONV, D_CONV, D_QK, D_QK, D_MLSTM, D_MLSTM, N_MLSTM_HEADS]).tolist()
    gb, gc, u, q, k, v, o, ig, fg = jnp.split(proj, idx, axis=-1)
    y_conv = gb * causal_dwconv3(gc * u, short_conv_w)
    def heads(a, d):
        return a.reshape(Bn, S, N_MLSTM_HEADS, d).transpose(0, 2, 1, 3)
    qh = heads(q, DQK_HEAD) * (DQK_HEAD ** -0.5)
    kh = heads(k, DQK_HEAD)
    vh = heads(v, DV_HEAD)
    igh = (ig.astype(jnp.float32) + b_igate.astype(jnp.float32)).transpose(0, 2, 1)
    fgh = (fg.astype(jnp.float32) + b_fgate.astype(jnp.float32)).transpose(0, 2, 1)
    hm = mlstm_chunkwise(qh, kh, vh, igh, fgh).transpose(0, 2, 1, 3)
    hm = hm * lax.rsqrt(jnp.mean(hm * hm, axis=-1, keepdims=True) + EPS)
    hm = hm.reshape(Bn, S, D_MLSTM) * mh_norm_g.astype(jnp.float32)
    y_m = (jax.nn.sigmoid(o.astype(jnp.float32)) * hm).astype(h.dtype)
    return jnp.concatenate([y_conv, y_m], axis=-1) @ w_out


def conv_ffn(h, w_up, ffn_conv_w, ffn_conv_b, w_down):
    up = causal_dwconv3(h @ w_up, ffn_conv_w) + ffn_conv_b
    g, u = jnp.split(up, 2, axis=-1)
    return (jax.nn.silu(g) * u) @ w_down


def setup_inputs(seed: int = 0) -> dict:
    key = jax.random.key(seed)
    ks = jax.random.split(key, 20)
    f32 = jnp.float32
    nrm = lambda k, shape, s: jax.random.normal(k, shape, f32) * s
    H = N_MLSTM_HEADS
    return {
        "x": nrm(ks[0], (BATCH, SEQ, D_MODEL), 1.0),
        "p": nrm(ks[1], (DEPTH, BATCH, SEQ, D_PLE), 1.0),
        "norm_mix_g": 1.0 + nrm(ks[2], (DEPTH, D_MODEL), 0.1),
        "w_in": nrm(ks[3], (DEPTH, D_MODEL, D_IN_PROJ), D_MODEL ** -0.5),
        "b_igate": nrm(ks[4], (DEPTH, H), 0.1) - 1.0,
        "b_fgate": nrm(ks[5], (DEPTH, H), 0.5) + 3.0,
        "short_conv_w": nrm(ks[6], (DEPTH, CONV_WIDTH, D_CONV), CONV_WIDTH ** -0.5),
        "mh_norm_g": 1.0 + nrm(ks[7], (DEPTH, D_MLSTM), 0.1),
        "w_out": nrm(ks[8], (DEPTH, D_MIX, D_MODEL), D_MIX ** -0.5),
        "norm_ffn_g": 1.0 + nrm(ks[9], (DEPTH, D_MODEL), 0.1),
        "w_up": nrm(ks[10], (DEPTH, D_MODEL, 2 * D_FF), D_MODEL ** -0.5),
        "ffn_conv_w": nrm(ks[11], (DEPTH, CONV_WIDTH, 2 * D_FF), CONV_WIDTH ** -0.5),
        "ffn_conv_b": nrm(ks[12], (DEPTH, 2 * D_FF), 0.01),
        "w_down": nrm(ks[13], (DEPTH, D_FF, D_MODEL), D_FF ** -0.5),
        "norm_ple_g": 1.0 + nrm(ks[14], (DEPTH, D_MODEL), 0.1),
        "w_ple_gate": nrm(ks[15], (DEPTH, D_MODEL, D_MODEL), D_MODEL ** -0.5),
        "w_ple_proj": nrm(ks[16], (DEPTH, D_PLE, D_MODEL), D_PLE ** -0.5),
        "final_norm_g": 1.0 + nrm(ks[17], (D_MODEL,), 0.1),
    }


def reference(x, p, norm_mix_g, w_in, b_igate, b_fgate, short_conv_w, mh_norm_g, w_out,
              norm_ffn_g, w_up, ffn_conv_w, ffn_conv_b, w_down, norm_ple_g, w_ple_gate,
              w_ple_proj, final_norm_g):
    for i in range(DEPTH):
        h = rmsnorm(x, norm_mix_g[i])
        x = x + hybrid_mixer(h, w_in[i], b_igate[i], b_fgate[i], short_conv_w[i],
                             mh_norm_g[i], w_out[i])
        h = rmsnorm(x, norm_ffn_g[i])
        x = x + conv_ffn(h, w_up[i], ffn_conv_w[i], ffn_conv_b[i], w_down[i])
        h = rmsnorm(x, norm_ple_g[i])
        gate = jax.nn.sigmoid((h @ w_ple_gate[i]).astype(jnp.float32)).astype(x.dtype)
        x = x + gate * (p[i] @ w_ple_proj[i])
    return rmsnorm(x, final_norm_g)
```

```python
import functools

import jax
import jax.numpy as jnp
from jax import lax
from jax.experimental import pallas as pl
from jax.experimental.pallas import tpu as pltpu

N_HEADS = 4
CONV_WIDTH = 3
EPS = 1e-6
LANES = 128
SUBLANES = 8
MLSTM_CHUNK = 256
VMEM_LIMIT = 56 * 1024 * 1024

_BF16 = jnp.bfloat16
_F32 = jnp.float32


def _dot(a, b):
    return jnp.dot(a, b, preferred_element_type=_F32)


def _rmsnorm_rows(x, g):
    ms = jnp.mean(x * x, axis=-1, keepdims=True)
    return x * lax.rsqrt(ms + EPS) * g


def _causal_conv3(z, cw, buf_ref, halo_ref, j, first_tile):
    tm = z.shape[0]

    @pl.when(first_tile)
    def _():
        buf_ref[0:SUBLANES, :] = jnp.zeros((SUBLANES, z.shape[1]), _F32)

    @pl.when(jnp.logical_not(first_tile))
    def _():
        buf_ref[0:SUBLANES, :] = halo_ref[j]

    buf_ref[SUBLANES:, :] = z
    halo_ref[j] = z[tm - SUBLANES:, :]
    z2 = buf_ref[SUBLANES - 2:SUBLANES - 2 + tm, :]
    z1 = buf_ref[SUBLANES - 1:SUBLANES - 1 + tm, :]
    return z2 * cw[0:1, :] + z1 * cw[1:2, :] + z * cw[2:3, :]


def _conv_group_kernel(x_ref, g_ref, wb_ref, wc_ref, wu_ref, cw_ref, o_ref,
                       h_ref, buf_ref, halo_ref, *, tiles_per_seq):
    m = pl.program_id(0)
    j = pl.program_id(1)

    @pl.when(j == 0)
    def _():
        h_ref[...] = _rmsnorm_rows(x_ref[...], g_ref[...]).astype(_BF16)

    h = h_ref[...]
    gb = _dot(h, wb_ref[...])
    z = _dot(h, wc_ref[...]) * _dot(h, wu_ref[...])
    y = _causal_conv3(z, cw_ref[...], buf_ref, halo_ref, j, m % tiles_per_seq == 0)
    o_ref[...] = (gb * y).astype(o_ref.dtype)


def _conv_group(x2d, g, wb, wc, wu, cw, *, seq, tm=512, tc=512):
    M, D = x2d.shape
    C = wb.shape[1]
    nj = C // tc
    wspec = pl.BlockSpec((D, tc), lambda m, j: (0, j))
    return pl.pallas_call(
        functools.partial(_conv_group_kernel, tiles_per_seq=seq // tm),
        grid=(M // tm, nj),
        in_specs=[
            pl.BlockSpec((tm, D), lambda m, j: (m, 0)),
            pl.BlockSpec((1, D), lambda m, j: (0, 0)),
            wspec, wspec, wspec,
            pl.BlockSpec((CONV_WIDTH, tc), lambda m, j: (0, j)),
        ],
        out_specs=pl.BlockSpec((tm, tc), lambda m, j: (m, j)),
        out_shape=jax.ShapeDtypeStruct((M, C), _BF16),
        scratch_shapes=[
            pltpu.VMEM((tm, D), _BF16),
            pltpu.VMEM((tm + SUBLANES, tc), _F32),
            pltpu.VMEM((nj, SUBLANES, tc), _F32),
        ],
        compiler_params=pltpu.CompilerParams(
            dimension_semantics=("arbitrary", "arbitrary"), vmem_limit_bytes=VMEM_LIMIT),
        name="conv_group",
    )(x2d, g, wb, wc, wu, cw)


def _qkvo_kernel(x_ref, g_ref, w_ref, cs_ref, wgate_ref, o_ref, gate_ref, h_ref):
    j = pl.program_id(1)

    @pl.when(j == 0)
    def _():
        h = _rmsnorm_rows(x_ref[...], g_ref[...]).astype(_BF16)
        h_ref[...] = h
        gate_ref[...] = _dot(h, wgate_ref[...])

    o_ref[...] = (_dot(h_ref[...], w_ref[...]) * cs_ref[...]).astype(o_ref.dtype)


def _qkvo(x2d, g, w, colscale, wgate, *, tm=512, tn=1024):
    M, D = x2d.shape
    N = w.shape[1]
    return pl.pallas_call(
        _qkvo_kernel,
        grid=(M // tm, N // tn),
        in_specs=[
            pl.BlockSpec((tm, D), lambda m, j: (m, 0)),
            pl.BlockSpec((1, D), lambda m, j: (0, 0)),
            pl.BlockSpec((D, tn), lambda m, j: (0, j)),
            pl.BlockSpec((1, tn), lambda m, j: (0, j)),
            pl.BlockSpec((D, LANES), lambda m, j: (0, 0)),
        ],
        out_specs=[
            pl.BlockSpec((tm, tn), lambda m, j: (m, j)),
            pl.BlockSpec((tm, LANES), lambda m, j: (m, 0)),
        ],
        out_shape=[
            jax.ShapeDtypeStruct((M, N), _BF16),
            jax.ShapeDtypeStruct((M, LANES), _F32),
        ],
        scratch_shapes=[pltpu.VMEM((tm, D), _BF16)],
        compiler_params=pltpu.CompilerParams(
            dimension_semantics=("arbitrary", "arbitrary"), vmem_limit_bytes=VMEM_LIMIT),
        name="qkvo",
    )(x2d, g, w, colscale, wgate)


def _exact_cumsum_rows(tril, a):
    hi = a.astype(_BF16)
    r1 = a - hi.astype(_F32)
    mid = r1.astype(_BF16)
    lo = (r1 - mid.astype(_F32)).astype(_BF16)
    return _dot(tril, hi) + _dot(tril, mid) + _dot(tril, lo)


def _mlstm_kernel(q_ref, k_ref, v_ref, o_ref, gate_ref, bias_ref, gn_ref, y_ref,
                  c_ref, n_ref, m_ref, *, dqk, dv):
    L = q_ref.shape[0]

    @pl.when(pl.program_id(1) == 0)
    def _():
        c_ref[...] = jnp.zeros(c_ref.shape, _F32)
        n_ref[...] = jnp.zeros(n_ref.shape, _F32)
        m_ref[...] = jnp.zeros(m_ref.shape, _F32)

    gates = gate_ref[...] + bias_ref[...]
    logf = jnp.minimum(gates, 0.0) - jnp.log1p(jnp.exp(-jnp.abs(gates)))
    row = lax.broadcasted_iota(jnp.int32, (L, L), 0)
    col = lax.broadcasted_iota(jnp.int32, (L, L), 1)
    causal = col <= row
    bcum = _exact_cumsum_rows(causal.astype(_BF16), logf)
    gates_t = gates.T
    bcum_t = bcum.T

    for h in range(N_HEADS):
        b_col = bcum[:, N_HEADS + h:N_HEADS + h + 1]
        li_col = gates[:, h:h + 1]
        b_row = bcum_t[N_HEADS + h:N_HEADS + h + 1, :]
        li_row = gates_t[h:h + 1, :]
        m_prev = m_ref[h:h + 1, 0:1]

        dlog = jnp.where(causal, b_col - b_row + li_row, -jnp.inf)
        inter = b_col + m_prev
        m_t = jnp.maximum(inter, jnp.max(dlog, axis=-1, keepdims=True))
        dw = jnp.exp(dlog - m_t)
        a_inter = jnp.exp(inter - m_t)

        q = q_ref[:, h * dqk:(h + 1) * dqk]
        k = k_ref[:, h * dqk:(h + 1) * dqk]
        v = v_ref[:, h * dv:(h + 1) * dv]
        c_prev = c_ref[h]
        n_prev = n_ref[h:h + 1, :]

        s = lax.dot_general(q, k, (((1,), (1,)), ((), ())),
                            preferred_element_type=_F32) * dw
        num = a_inter * _dot(q, c_prev.astype(_BF16)) + _dot(s.astype(_BF16), v)
        qn = jnp.sum(q.astype(_F32) * n_prev, axis=-1, keepdims=True)
        den = a_inter * qn + jnp.sum(s, axis=-1, keepdims=True)
        den = jnp.maximum(jnp.abs(den), jnp.exp(-m_t))
        hm = num / den
        hm = hm * lax.rsqrt(jnp.mean(hm * hm, axis=-1, keepdims=True) + EPS)
        hm = hm * gn_ref[:, h * dv:(h + 1) * dv]
        og = jax.nn.sigmoid(o_ref[:, h * dv:(h + 1) * dv].astype(_F32))
        y_ref[:, h * dv:(h + 1) * dv] = (og * hm).astype(y_ref.dtype)

        b_tot = b_col[L - 1:L, :]
        m_new = m_t[L - 1:L, :]
        a_prev = a_inter[L - 1:L, :]
        w_col = jnp.exp(b_tot - b_col + li_col - m_new)
        kw = k.astype(_F32) * w_col
        c_ref[h] = a_prev * c_prev + _dot(kw.T.astype(_BF16), v)
        n_ref[h:h + 1, :] = a_prev * n_prev + jnp.sum(kw, axis=0, keepdims=True)
        m_ref[h:h + 1, :] = jnp.broadcast_to(m_new, (1, LANES))


def _mlstm(qkvo, gates, gate_bias, gnorm, *, batch, seq, dqk, dv):
    M = qkvo.shape[0]
    L = MLSTM_CHUNK
    nc = seq // L
    dq_all = N_HEADS * dqk
    dv_all = N_HEADS * dv
    assert dv_all == 2 * dq_all
    tok = lambda b, c: b * nc + c
    return pl.pallas_call(
        functools.partial(_mlstm_kernel, dqk=dqk, dv=dv),
        grid=(batch, nc),
        in_specs=[
            pl.BlockSpec((L, dq_all), lambda b, c: (tok(b, c), 0)),
            pl.BlockSpec((L, dq_all), lambda b, c: (tok(b, c), 1)),
            pl.BlockSpec((L, dv_all), lambda b, c: (tok(b, c), 1)),
            pl.BlockSpec((L, dv_all), lambda b, c: (tok(b, c), 2)),
            pl.BlockSpec((L, LANES), lambda b, c: (tok(b, c), 0)),
            pl.BlockSpec((1, LANES), lambda b, c: (0, 0)),
            pl.BlockSpec((1, dv_all), lambda b, c: (0, 0)),
        ],
        out_specs=pl.BlockSpec((L, dv_all), lambda b, c: (tok(b, c), 0)),
        out_shape=jax.ShapeDtypeStruct((M, dv_all), _BF16),
        scratch_shapes=[
            pltpu.VMEM((N_HEADS, dqk, dv), _F32),
            pltpu.VMEM((SUBLANES, dqk), _F32),
            pltpu.VMEM((SUBLANES, LANES), _F32),
        ],
        compiler_params=pltpu.CompilerParams(
            dimension_semantics=("arbitrary", "arbitrary"), vmem_limit_bytes=VMEM_LIMIT),
        name="mlstm",
    )(qkvo, qkvo, qkvo, qkvo, gates, gate_bias, gnorm)


def _out_proj_kernel(x_ref, yc_ref, ym_ref, wc_ref, wm_ref, o_ref):
    acc = _dot(yc_ref[...], wc_ref[...]) + _dot(ym_ref[...], wm_ref[...])
    o_ref[...] = x_ref[...] + acc


def _out_proj(x2d, yc, ym, w_out, *, tm=1024, tn=1024):
    M, D = x2d.shape
    kc = yc.shape[1]
    km = ym.shape[1]
    assert kc == km
    return pl.pallas_call(
        _out_proj_kernel,
        grid=(M // tm, D // tn),
        in_specs=[
            pl.BlockSpec((tm, tn), lambda m, j: (m, j)),
            pl.BlockSpec((tm, kc), lambda m, j: (m, 0)),
            pl.BlockSpec((tm, km), lambda m, j: (m, 0)),
            pl.BlockSpec((kc, tn), lambda m, j: (0, j)),
            pl.BlockSpec((km, tn), lambda m, j: (1, j)),
        ],
        out_specs=pl.BlockSpec((tm, tn), lambda m, j: (m, j)),
        out_shape=jax.ShapeDtypeStruct((M, D), _F32),
        compiler_params=pltpu.CompilerParams(
            dimension_semantics=("arbitrary", "arbitrary"), vmem_limit_bytes=VMEM_LIMIT),
        name="out_proj",
    )(x2d, yc, ym, w_out, w_out)


def _ffn_kernel(x_ref, g_ref, wg_ref, wu_ref, cwg_ref, cwu_ref, bg_ref, bu_ref, wd_ref,
                o_ref, h_ref, bufg_ref, bufu_ref, halog_ref, halou_ref, *, tiles_per_seq):
    m = pl.program_id(0)
    j = pl.program_id(1)

    @pl.when(j == 0)
    def _():
        x = x_ref[...]
        h_ref[...] = _rmsnorm_rows(x, g_ref[...]).astype(_BF16)
        o_ref[...] = x

    h = h_ref[...]
    first = m % tiles_per_seq == 0
    gate = _causal_conv3(_dot(h, wg_ref[...]), cwg_ref[...], bufg_ref, halog_ref, j, first)
    up = _causal_conv3(_dot(h, wu_ref[...]), cwu_ref[...], bufu_ref, halou_ref, j, first)
    gate = gate + bg_ref[...]
    up = up + bu_ref[...]
    act = (jax.nn.silu(gate) * up).astype(_BF16)
    o_ref[...] += _dot(act, wd_ref[...])


def _ffn(x2d, g, w_up, cw, cb, w_down, *, seq, tm=512, tf=512):
    M, D = x2d.shape
    F = w_down.shape[0]
    nj = F // tf
    return pl.pallas_call(
        functools.partial(_ffn_kernel, tiles_per_seq=seq // tm),
        grid=(M // tm, nj),
        in_specs=[
            pl.BlockSpec((tm, D), lambda m, j: (m, 0)),
            pl.BlockSpec((1, D), lambda m, j: (0, 0)),
            pl.BlockSpec((D, tf), lambda m, j: (0, j)),
            pl.BlockSpec((D, tf), lambda m, j: (0, nj + j)),
            pl.BlockSpec((CONV_WIDTH, tf), lambda m, j: (0, j)),
            pl.BlockSpec((CONV_WIDTH, tf), lambda m, j: (0, nj + j)),
            pl.BlockSpec((1, tf), lambda m, j: (0, j)),
            pl.BlockSpec((1, tf), lambda m, j: (0, nj + j)),
            pl.BlockSpec((tf, D), lambda m, j: (j, 0)),
        ],
        out_specs=pl.BlockSpec((tm, D), lambda m, j: (m, 0)),
        out_shape=jax.ShapeDtypeStruct((M, D), _F32),
        scratch_shapes=[
            pltpu.VMEM((tm, D), _BF16),
            pltpu.VMEM((tm + SUBLANES, tf), _F32),
            pltpu.VMEM((tm + SUBLANES, tf), _F32),
            pltpu.VMEM((nj, SUBLANES, tf), _F32),
            pltpu.VMEM((nj, SUBLANES, tf), _F32),
        ],
        compiler_params=pltpu.CompilerParams(
            dimension_semantics=("arbitrary", "arbitrary"), vmem_limit_bytes=VMEM_LIMIT),
        name="ffn",
    )(x2d, g, w_up, w_up, cw, cw, cb, cb, w_down)


def _ple_kernel(x_ref, g_ref, p_ref, wg_ref, wp_ref, gf_ref, o_ref, *, final_norm):
    x = x_ref[...]
    h = _rmsnorm_rows(x, g_ref[...]).astype(_BF16)
    gate = jax.nn.sigmoid(_dot(h, wg_ref[...]))
    x = x + gate * _dot(p_ref[...].astype(_BF16), wp_ref[...])
    if final_norm:
        x = _rmsnorm_rows(x, gf_ref[...])
    o_ref[...] = x


def _ple(x2d, g, p2d, w_gate, w_proj, g_final, *, final_norm, tm=512):
    M, D = x2d.shape
    P = p2d.shape[1]
    return pl.pallas_call(
        functools.partial(_ple_kernel, final_norm=final_norm),
        grid=(M // tm,),
        in_specs=[
            pl.BlockSpec((tm, D), lambda m: (m, 0)),
            pl.BlockSpec((1, D), lambda m: (0, 0)),
            pl.BlockSpec((tm, P), lambda m: (m, 0)),
            pl.BlockSpec((D, D), lambda m: (0, 0)),
            pl.BlockSpec((P, D), lambda m: (0, 0)),
            pl.BlockSpec((1, D), lambda m: (0, 0)),
        ],
        out_specs=pl.BlockSpec((tm, D), lambda m: (m, 0)),
        out_shape=jax.ShapeDtypeStruct((M, D), _F32),
        compiler_params=pltpu.CompilerParams(
            dimension_semantics=("arbitrary",), vmem_limit_bytes=VMEM_LIMIT),
        name="ple",
    )(x2d, g, p2d, w_gate, w_proj, g_final)


def kernel(x, p, norm_mix_g, w_in, b_igate, b_fgate, short_conv_w, mh_norm_g, w_out,
           norm_ffn_g, w_up, ffn_conv_w, ffn_conv_b, w_down, norm_ple_g, w_ple_gate,
           w_ple_proj, final_norm_g):
    B, S, D = x.shape
    depth = w_in.shape[0]
    d_conv = short_conv_w.shape[-1]
    d_mlstm = mh_norm_g.shape[-1]
    dv = d_mlstm // N_HEADS
    d_qk = (w_in.shape[-1] - 3 * d_conv - 2 * d_mlstm - 2 * N_HEADS) // 2
    dqk = d_qk // N_HEADS
    M = B * S

    x2d = x.reshape(M, D)
    row = lambda a: a.reshape(1, -1).astype(_F32)
    colscale = jnp.concatenate([
        jnp.full((1, d_qk), dqk ** -0.5, _F32),
        jnp.ones((1, d_qk + 2 * d_mlstm), _F32)], axis=1)

    for i in range(depth):
        w = w_in[i]
        c0 = 3 * d_conv
        c1 = c0 + 2 * d_qk + 2 * d_mlstm
        wb = w[:, 0:d_conv].astype(_BF16)
        wc = w[:, d_conv:2 * d_conv].astype(_BF16)
        wu = w[:, 2 * d_conv:c0].astype(_BF16)
        w_qkvo = w[:, c0:c1].astype(_BF16)
        w_gates = jnp.pad(w[:, c1:], ((0, 0), (0, LANES - 2 * N_HEADS))).astype(_BF16)
        gate_bias = jnp.pad(jnp.concatenate([b_igate[i], b_fgate[i]]).astype(_F32),
                            (0, LANES - 2 * N_HEADS)).reshape(1, LANES)

        g_mix = row(norm_mix_g[i])
        y_conv = _conv_group(x2d, g_mix, wb, wc, wu, short_conv_w[i].astype(_F32), seq=S)
        qkvo, gates = _qkvo(x2d, g_mix, w_qkvo, colscale, w_gates)
        y_m = _mlstm(qkvo, gates, gate_bias, row(mh_norm_g[i]),
                     batch=B, seq=S, dqk=dqk, dv=dv)
        x2d = _out_proj(x2d, y_conv, y_m, w_out[i].astype(_BF16))
        x2d = _ffn(x2d, row(norm_ffn_g[i]), w_up[i].astype(_BF16),
                   ffn_conv_w[i].astype(_F32), row(ffn_conv_b[i]),
                   w_down[i].astype(_BF16), seq=S)
        x2d = _ple(x2d, row(norm_ple_g[i]), p[i].reshape(M, -1), w_ple_gate[i].astype(_BF16),
                   w_ple_proj[i].astype(_BF16), row(final_norm_g),
                   final_norm=(i == depth - 1))
    return x2d.reshape(B, S, D)
```

```python
import functools

import jax
import jax.numpy as jnp
from jax import lax
from jax.experimental import pallas as pl
from jax.experimental.pallas import tpu as pltpu

N_HEADS = 4
CONV_WIDTH = 3
EPS = 1e-6
LANES = 128
SUBLANES = 8
MLSTM_CHUNK = 256
W_COLS = 512
VMEM_LIMIT = 56 * 1024 * 1024

_BF16 = jnp.bfloat16
_F32 = jnp.float32


def _dot(a, b):
    return jnp.dot(a, b, preferred_element_type=_F32)


def _col_blocks(w):
    k, n = w.shape
    return w.astype(_BF16).reshape(k, n // W_COLS, W_COLS).transpose(1, 0, 2)


def _rmsnorm_rows(x, g):
    ms = jnp.mean(x * x, axis=-1, keepdims=True)
    return x * lax.rsqrt(ms + EPS) * g


def _causal_conv3(z, prev, cw):
    n = z.shape[0]
    ext = jnp.concatenate([prev, z], axis=0)
    z2 = ext[SUBLANES - 2:SUBLANES - 2 + n, :]
    z1 = ext[SUBLANES - 1:SUBLANES - 1 + n, :]
    return z2 * cw[0:1, :] + z1 * cw[1:2, :] + z * cw[2:3, :]


def _load_halo(halo_ref, j, first_tile):
    @pl.when(first_tile)
    def _():
        halo_ref[j] = jnp.zeros(halo_ref.shape[1:], _F32)

    return halo_ref[j]


def _conv_group_kernel(x_ref, g_ref, wb_ref, wc_ref, wu_ref, cw_ref, o_ref,
                       h_ref, halo_ref, *, tiles_per_seq):
    m = pl.program_id(0)
    j = pl.program_id(1)
    tm = x_ref.shape[0]

    @pl.when(j == 0)
    def _():
        h_ref[...] = _rmsnorm_rows(x_ref[...], g_ref[...]).astype(_BF16)

    h = h_ref[...]
    gb = _dot(h, wb_ref[...])
    z = _dot(h, wc_ref[...]) * _dot(h, wu_ref[...])
    y = _causal_conv3(z, _load_halo(halo_ref, j, m % tiles_per_seq == 0), cw_ref[...])
    halo_ref[j] = z[tm - SUBLANES:, :]
    o_ref[...] = (gb * y).astype(o_ref.dtype)


def _conv_group(x2d, g, wb, wc, wu, cw, *, seq, tm=512, tc=512):
    M, D = x2d.shape
    C = wb.shape[1]
    nj = C // tc
    wspec = pl.BlockSpec((D, tc), lambda m, j: (0, j))
    return pl.pallas_call(
        functools.partial(_conv_group_kernel, tiles_per_seq=seq // tm),
        grid=(M // tm, nj),
        in_specs=[
            pl.BlockSpec((tm, D), lambda m, j: (m, 0)),
            pl.BlockSpec((1, D), lambda m, j: (0, 0)),
            wspec, wspec, wspec,
            pl.BlockSpec((CONV_WIDTH, tc), lambda m, j: (0, j)),
        ],
        out_specs=pl.BlockSpec((tm, tc), lambda m, j: (m, j)),
        out_shape=jax.ShapeDtypeStruct((M, C), _BF16),
        scratch_shapes=[
            pltpu.VMEM((tm, D), _BF16),
            pltpu.VMEM((nj, SUBLANES, tc), _F32),
        ],
        compiler_params=pltpu.CompilerParams(
            dimension_semantics=("arbitrary", "arbitrary"), vmem_limit_bytes=VMEM_LIMIT),
        name="conv_group",
    )(x2d, g, wb, wc, wu, cw)


def _qkvo_kernel(x_ref, g_ref, w_ref, cs_ref, wgate_ref, o_ref, gate_ref, h_ref):
    j = pl.program_id(1)

    @pl.when(j == 0)
    def _():
        h = _rmsnorm_rows(x_ref[...], g_ref[...]).astype(_BF16)
        h_ref[...] = h
        gate_ref[...] = _dot(h, wgate_ref[...])

    o_ref[...] = (_dot(h_ref[...], w_ref[...]) * cs_ref[...]).astype(o_ref.dtype)


def _qkvo(x2d, g, w, colscale, wgate, *, tm=512, tn=1024):
    M, D = x2d.shape
    N = w.shape[1]
    return pl.pallas_call(
        _qkvo_kernel,
        grid=(M // tm, N // tn),
        in_specs=[
            pl.BlockSpec((tm, D), lambda m, j: (m, 0)),
            pl.BlockSpec((1, D), lambda m, j: (0, 0)),
            pl.BlockSpec((D, tn), lambda m, j: (0, j)),
            pl.BlockSpec((1, tn), lambda m, j: (0, j)),
            pl.BlockSpec((D, LANES), lambda m, j: (0, 0)),
        ],
        out_specs=[
            pl.BlockSpec((tm, tn), lambda m, j: (m, j)),
            pl.BlockSpec((tm, LANES), lambda m, j: (m, 0)),
        ],
        out_shape=[
            jax.ShapeDtypeStruct((M, N), _BF16),
            jax.ShapeDtypeStruct((M, LANES), _F32),
        ],
        scratch_shapes=[pltpu.VMEM((tm, D), _BF16)],
        compiler_params=pltpu.CompilerParams(
            dimension_semantics=("arbitrary", "arbitrary"), vmem_limit_bytes=VMEM_LIMIT),
        name="qkvo",
    )(x2d, g, w, colscale, wgate)


def _exact_cumsum_rows(tril, a):
    hi = a.astype(_BF16)
    r1 = a - hi.astype(_F32)
    mid = r1.astype(_BF16)
    lo = (r1 - mid.astype(_F32)).astype(_BF16)
    return _dot(tril, hi) + _dot(tril, mid) + _dot(tril, lo)


def _mlstm_kernel(q_ref, k_ref, v_ref, o_ref, gate_ref, bias_ref, gn_ref, y_ref,
                  c_ref, n_ref, m_ref, *, dqk, dv):
    L = q_ref.shape[0]

    @pl.when(pl.program_id(1) == 0)
    def _():
        c_ref[...] = jnp.zeros(c_ref.shape, _F32)
        n_ref[...] = jnp.zeros(n_ref.shape, _F32)
        m_ref[...] = jnp.zeros(m_ref.shape, _F32)

    gates = gate_ref[...] + bias_ref[...]
    logf = jnp.minimum(gates, 0.0) - jnp.log1p(jnp.exp(-jnp.abs(gates)))
    row = lax.broadcasted_iota(jnp.int32, (L, L), 0)
    col = lax.broadcasted_iota(jnp.int32, (L, L), 1)
    causal = col <= row
    bcum = _exact_cumsum_rows(causal.astype(_BF16), logf)
    gates_t = gates.T
    bcum_t = bcum.T

    for h in range(N_HEADS):
        b_col = bcum[:, N_HEADS + h:N_HEADS + h + 1]
        li_col = gates[:, h:h + 1]
        b_row = bcum_t[N_HEADS + h:N_HEADS + h + 1, :]
        li_row = gates_t[h:h + 1, :]
        m_prev = m_ref[h:h + 1, 0:1]

        dlog = jnp.where(causal, b_col - b_row + li_row, -jnp.inf)
        inter = b_col + m_prev
        m_t = jnp.maximum(inter, jnp.max(dlog, axis=-1, keepdims=True))
        dw = jnp.exp(dlog - m_t)
        a_inter = jnp.exp(inter - m_t)

        q = q_ref[:, h * dqk:(h + 1) * dqk]
        k = k_ref[:, h * dqk:(h + 1) * dqk]
        v = v_ref[:, h * dv:(h + 1) * dv]
        c_prev = c_ref[h]
        n_prev = n_ref[h:h + 1, :]

        s = lax.dot_general(q, k, (((1,), (1,)), ((), ())),
                            preferred_element_type=_F32) * dw
        num = a_inter * _dot(q, c_prev.astype(_BF16)) + _dot(s.astype(_BF16), v)
        qn = jnp.sum(q.astype(_F32) * n_prev, axis=-1, keepdims=True)
        den = a_inter * qn + jnp.sum(s, axis=-1, keepdims=True)
        den = jnp.maximum(jnp.abs(den), jnp.exp(-m_t))
        hm = num / den
        hm = hm * lax.rsqrt(jnp.mean(hm * hm, axis=-1, keepdims=True) + EPS)
        hm = hm * gn_ref[:, h * dv:(h + 1) * dv]
        og = jax.nn.sigmoid(o_ref[:, h * dv:(h + 1) * dv].astype(_F32))
        y_ref[:, h * dv:(h + 1) * dv] = (og * hm).astype(y_ref.dtype)

        b_tot = b_col[L - 1:L, :]
        m_new = m_t[L - 1:L, :]
        a_prev = a_inter[L - 1:L, :]
        w_col = jnp.exp(b_tot - b_col + li_col - m_new)
        kw = k.astype(_F32) * w_col
        c_ref[h] = a_prev * c_prev + _dot(kw.T.astype(_BF16), v)
        n_ref[h:h + 1, :] = a_prev * n_prev + jnp.sum(kw, axis=0, keepdims=True)
        m_ref[h:h + 1, :] = jnp.broadcast_to(m_new, (1, LANES))


def _mlstm(qkvo, gates, gate_bias, gnorm, *, batch, seq, dqk, dv):
    M = qkvo.shape[0]
    L = MLSTM_CHUNK
    nc = seq // L
    dq_all = N_HEADS * dqk
    dv_all = N_HEADS * dv
    assert dv_all == 2 * dq_all
    tok = lambda b, c: b * nc + c
    return pl.pallas_call(
        functools.partial(_mlstm_kernel, dqk=dqk, dv=dv),
        grid=(batch, nc),
        in_specs=[
            pl.BlockSpec((L, dq_all), lambda b, c: (tok(b, c), 0)),
            pl.BlockSpec((L, dq_all), lambda b, c: (tok(b, c), 1)),
            pl.BlockSpec((L, dv_all), lambda b, c: (tok(b, c), 1)),
            pl.BlockSpec((L, dv_all), lambda b, c: (tok(b, c), 2)),
            pl.BlockSpec((L, LANES), lambda b, c: (tok(b, c), 0)),
            pl.BlockSpec((1, LANES), lambda b, c: (0, 0)),
            pl.BlockSpec((1, dv_all), lambda b, c: (0, 0)),
        ],
        out_specs=pl.BlockSpec((L, dv_all), lambda b, c: (tok(b, c), 0)),
        out_shape=jax.ShapeDtypeStruct((M, dv_all), _BF16),
        scratch_shapes=[
            pltpu.VMEM((N_HEADS, dqk, dv), _F32),
            pltpu.VMEM((SUBLANES, dqk), _F32),
            pltpu.VMEM((SUBLANES, LANES), _F32),
        ],
        compiler_params=pltpu.CompilerParams(
            dimension_semantics=("arbitrary", "arbitrary"), vmem_limit_bytes=VMEM_LIMIT),
        name="mlstm",
    )(qkvo, qkvo, qkvo, qkvo, gates, gate_bias, gnorm)


def _out_proj_kernel(x_ref, yc_ref, ym_ref, wc_ref, wm_ref, o_ref):
    acc = _dot(yc_ref[...], wc_ref[...]) + _dot(ym_ref[...], wm_ref[...])
    o_ref[...] = x_ref[...] + acc


def _out_proj(x2d, yc, ym, w_out, *, tm=1024, tn=1024):
    M, D = x2d.shape
    kc = yc.shape[1]
    km = ym.shape[1]
    assert kc == km
    return pl.pallas_call(
        _out_proj_kernel,
        grid=(M // tm, D // tn),
        in_specs=[
            pl.BlockSpec((tm, tn), lambda m, j: (m, j)),
            pl.BlockSpec((tm, kc), lambda m, j: (m, 0)),
            pl.BlockSpec((tm, km), lambda m, j: (m, 0)),
            pl.BlockSpec((kc, tn), lambda m, j: (0, j)),
            pl.BlockSpec((km, tn), lambda m, j: (1, j)),
        ],
        out_specs=pl.BlockSpec((tm, tn), lambda m, j: (m, j)),
        out_shape=jax.ShapeDtypeStruct((M, D), _F32),
        compiler_params=pltpu.CompilerParams(
            dimension_semantics=("arbitrary", "arbitrary"), vmem_limit_bytes=VMEM_LIMIT),
        name="out_proj",
    )(x2d, yc, ym, w_out, w_out)


def _ffn_kernel(x_ref, g_ref, wg_ref, wu_ref, cwg_ref, cwu_ref, bg_ref, bu_ref, wd_ref,
                o_ref, h_ref, halog_ref, halou_ref, *, tiles_per_seq, rows):
    m = pl.program_id(0)
    j = pl.program_id(1)
    tm = x_ref.shape[0]

    @pl.when(j == 0)
    def _():
        x = x_ref[...]
        h_ref[...] = _rmsnorm_rows(x, g_ref[...]).astype(_BF16)
        o_ref[...] = x

    first = m % tiles_per_seq == 0
    prev_g = _load_halo(halog_ref, j, first)
    prev_u = _load_halo(halou_ref, j, first)

    def up_proj(r):
        h = h_ref[r * rows:(r + 1) * rows, :]
        return _dot(h, wg_ref[...]), _dot(h, wu_ref[...])

    n_chunks = tm // rows
    cur = up_proj(0)
    for r in range(n_chunks):
        nxt = up_proj(r + 1) if r + 1 < n_chunks else None
        zg, zu = cur
        gate = _causal_conv3(zg, prev_g, cwg_ref[...]) + bg_ref[...]
        up = _causal_conv3(zu, prev_u, cwu_ref[...]) + bu_ref[...]
        act = (jax.nn.silu(gate) * up).astype(_BF16)
        for c in range(wd_ref.shape[0]):
            o_ref[r * rows:(r + 1) * rows, c * W_COLS:(c + 1) * W_COLS] += _dot(act, wd_ref[c])
        prev_g = zg[rows - SUBLANES:, :]
        prev_u = zu[rows - SUBLANES:, :]
        cur = nxt
    halog_ref[j] = prev_g
    halou_ref[j] = prev_u


def _ffn(x2d, g, w_up, cw, cb, w_down, *, seq, tm=1024, tf=512, rows=256):
    M, D = x2d.shape
    nb, F, _ = w_down.shape
    nj = F // tf
    return pl.pallas_call(
        functools.partial(_ffn_kernel, tiles_per_seq=seq // tm, rows=rows),
        grid=(M // tm, nj),
        in_specs=[
            pl.BlockSpec((tm, D), lambda m, j: (m, 0), pipeline_mode=pl.Buffered(1)),
            pl.BlockSpec((1, D), lambda m, j: (0, 0)),
            pl.BlockSpec((D, tf), lambda m, j: (0, j)),
            pl.BlockSpec((D, tf), lambda m, j: (0, nj + j)),
            pl.BlockSpec((CONV_WIDTH, tf), lambda m, j: (0, j)),
            pl.BlockSpec((CONV_WIDTH, tf), lambda m, j: (0, nj + j)),
            pl.BlockSpec((1, tf), lambda m, j: (0, j)),
            pl.BlockSpec((1, tf), lambda m, j: (0, nj + j)),
            pl.BlockSpec((nb, tf, W_COLS), lambda m, j: (0, j, 0)),
        ],
        out_specs=pl.BlockSpec((tm, D), lambda m, j: (m, 0)),
        out_shape=jax.ShapeDtypeStruct((M, D), _F32),
        scratch_shapes=[
            pltpu.VMEM((tm, D), _BF16),
            pltpu.VMEM((nj, SUBLANES, tf), _F32),
            pltpu.VMEM((nj, SUBLANES, tf), _F32),
        ],
        compiler_params=pltpu.CompilerParams(
            dimension_semantics=("arbitrary", "arbitrary"), vmem_limit_bytes=VMEM_LIMIT),
        name="ffn",
    )(x2d, g, w_up, w_up, cw, cw, cb, cb, w_down)


def _ple_kernel(x_ref, g_ref, p_ref, wg_ref, wp_ref, gf_ref, o_ref, *, final_norm):
    x = x_ref[...]
    h = _rmsnorm_rows(x, g_ref[...]).astype(_BF16)
    gate = jax.nn.sigmoid(_dot(h, wg_ref[...]))
    x = x + gate * _dot(p_ref[...].astype(_BF16), wp_ref[...])
    if final_norm:
        x = _rmsnorm_rows(x, gf_ref[...])
    o_ref[...] = x


def _ple(x2d, g, p2d, w_gate, w_proj, g_final, *, final_norm, tm=512):
    M, D = x2d.shape
    P = p2d.shape[1]
    return pl.pallas_call(
        functools.partial(_ple_kernel, final_norm=final_norm),
        grid=(M // tm,),
        in_specs=[
            pl.BlockSpec((tm, D), lambda m: (m, 0)),
            pl.BlockSpec((1, D), lambda m: (0, 0)),
            pl.BlockSpec((tm, P), lambda m: (m, 0)),
            pl.BlockSpec((D, D), lambda m: (0, 0)),
            pl.BlockSpec((P, D), lambda m: (0, 0)),
            pl.BlockSpec((1, D), lambda m: (0, 0)),
        ],
        out_specs=pl.BlockSpec((tm, D), lambda m: (m, 0)),
        out_shape=jax.ShapeDtypeStruct((M, D), _F32),
        compiler_params=pltpu.CompilerParams(
            dimension_semantics=("arbitrary",), vmem_limit_bytes=VMEM_LIMIT),
        name="ple",
    )(x2d, g, p2d, w_gate, w_proj, g_final)


def kernel(x, p, norm_mix_g, w_in, b_igate, b_fgate, short_conv_w, mh_norm_g, w_out,
           norm_ffn_g, w_up, ffn_conv_w, ffn_conv_b, w_down, norm_ple_g, w_ple_gate,
           w_ple_proj, final_norm_g):
    B, S, D = x.shape
    depth = w_in.shape[0]
    d_conv = short_conv_w.shape[-1]
    d_mlstm = mh_norm_g.shape[-1]
    dv = d_mlstm // N_HEADS
    d_qk = (w_in.shape[-1] - 3 * d_conv - 2 * d_mlstm - 2 * N_HEADS) // 2
    dqk = d_qk // N_HEADS
    M = B * S

    x2d = x.reshape(M, D)
    row = lambda a: a.reshape(1, -1).astype(_F32)
    colscale = jnp.concatenate([
        jnp.full((1, d_qk), dqk ** -0.5, _F32),
        jnp.ones((1, d_qk + 2 * d_mlstm), _F32)], axis=1)

    for i in range(depth):
        w = w_in[i]
        c0 = 3 * d_conv
        c1 = c0 + 2 * d_qk + 2 * d_mlstm
        wb = w[:, 0:d_conv].astype(_BF16)
        wc = w[:, d_conv:2 * d_conv].astype(_BF16)
        wu = w[:, 2 * d_conv:c0].astype(_BF16)
        w_qkvo = w[:, c0:c1].astype(_BF16)
        w_gates = jnp.pad(w[:, c1:], ((0, 0), (0, LANES - 2 * N_HEADS))).astype(_BF16)
        gate_bias = jnp.pad(jnp.concatenate([b_igate[i], b_fgate[i]]).astype(_F32),
                            (0, LANES - 2 * N_HEADS)).reshape(1, LANES)

        g_mix = row(norm_mix_g[i])
        y_conv = _conv_group(x2d, g_mix, wb, wc, wu, short_conv_w[i].astype(_F32), seq=S)
        qkvo, gates = _qkvo(x2d, g_mix, w_qkvo, colscale, w_gates)
        y_m = _mlstm(qkvo, gates, gate_bias, row(mh_norm_g[i]),
                     batch=B, seq=S, dqk=dqk, dv=dv)
        x2d = _out_proj(x2d, y_conv, y_m, w_out[i].astype(_BF16))
        x2d = _ffn(x2d, row(norm_ffn_g[i]), w_up[i].astype(_BF16),
                   ffn_conv_w[i].astype(_F32), row(ffn_conv_b[i]),
                   _col_blocks(w_down[i]), seq=S)
        x2d = _ple(x2d, row(norm_ple_g[i]), p[i].reshape(M, -1), w_ple_gate[i].astype(_BF16),
                   w_ple_proj[i].astype(_BF16), row(final_norm_g),
                   final_norm=(i == depth - 1))
    return x2d.reshape(B, S, D)
```

```python
import functools

import jax
import jax.numpy as jnp
from jax import lax
from jax.experimental import pallas as pl
from jax.experimental.pallas import tpu as pltpu

N_HEADS = 4
CONV_WIDTH = 3
EPS = 1e-6
LANES = 128
SUBLANES = 8
MLSTM_CHUNK = 256
W_COLS = 512
VMEM_LIMIT = 56 * 1024 * 1024

_BF16 = jnp.bfloat16
_F32 = jnp.float32


def _dot(a, b):
    return jnp.dot(a, b, preferred_element_type=_F32)


def _col_blocks(w):
    k, n = w.shape
    return w.astype(_BF16).reshape(k, n // W_COLS, W_COLS).transpose(1, 0, 2)


def _rmsnorm_rows(x, g):
    ms = jnp.mean(x * x, axis=-1, keepdims=True)
    return x * lax.rsqrt(ms + EPS) * g


def _causal_conv3(z, prev, cw):
    n = z.shape[0]
    ext = jnp.concatenate([prev, z], axis=0)
    z2 = ext[SUBLANES - 2:SUBLANES - 2 + n, :]
    z1 = ext[SUBLANES - 1:SUBLANES - 1 + n, :]
    return z2 * cw[0:1, :] + z1 * cw[1:2, :] + z * cw[2:3, :]


def _load_halo(halo_ref, j, first_tile):
    @pl.when(first_tile)
    def _():
        halo_ref[j] = jnp.zeros(halo_ref.shape[1:], _F32)

    return halo_ref[j]


def _conv_group_kernel(x_ref, g_ref, wb_ref, wc_ref, wu_ref, cw_ref, o_ref,
                       h_ref, halo_ref, *, tiles_per_seq, rows):
    m = pl.program_id(0)
    j = pl.program_id(1)
    tm = x_ref.shape[0]

    @pl.when(j == 0)
    def _():
        h_ref[...] = _rmsnorm_rows(x_ref[...], g_ref[...]).astype(_BF16)

    def proj(r):
        h = h_ref[r * rows:(r + 1) * rows, :]
        return _dot(h, wb_ref[...]), _dot(h, wc_ref[...]), _dot(h, wu_ref[...])

    prev = _load_halo(halo_ref, j, m % tiles_per_seq == 0)
    n_chunks = tm // rows
    cur = proj(0)
    for r in range(n_chunks):
        nxt = proj(r + 1) if r + 1 < n_chunks else None
        gb, gc, u = cur
        z = gc * u
        y = _causal_conv3(z, prev, cw_ref[...])
        o_ref[r * rows:(r + 1) * rows, :] = (gb * y).astype(o_ref.dtype)
        prev = z[rows - SUBLANES:, :]
        cur = nxt
    halo_ref[j] = prev


def _conv_group(x2d, g, wb, wc, wu, cw, *, seq, tm=1024, tc=512, rows=256):
    M, D = x2d.shape
    C = wb.shape[1]
    nj = C // tc
    wspec = pl.BlockSpec((D, tc), lambda m, j: (0, j))
    return pl.pallas_call(
        functools.partial(_conv_group_kernel, tiles_per_seq=seq // tm, rows=rows),
        grid=(M // tm, nj),
        in_specs=[
            pl.BlockSpec((tm, D), lambda m, j: (m, 0)),
            pl.BlockSpec((1, D), lambda m, j: (0, 0)),
            wspec, wspec, wspec,
            pl.BlockSpec((CONV_WIDTH, tc), lambda m, j: (0, j)),
        ],
        out_specs=pl.BlockSpec((tm, tc), lambda m, j: (m, j)),
        out_shape=jax.ShapeDtypeStruct((M, C), _BF16),
        scratch_shapes=[
            pltpu.VMEM((tm, D), _BF16),
            pltpu.VMEM((nj, SUBLANES, tc), _F32),
        ],
        compiler_params=pltpu.CompilerParams(
            dimension_semantics=("arbitrary", "arbitrary"), vmem_limit_bytes=VMEM_LIMIT),
        name="conv_group",
    )(x2d, g, wb, wc, wu, cw)


def _qkvo_kernel(x_ref, g_ref, w_ref, cs_ref, wgate_ref, o_ref, gate_ref, h_ref):
    j = pl.program_id(1)

    @pl.when(j == 0)
    def _():
        h = _rmsnorm_rows(x_ref[...], g_ref[...]).astype(_BF16)
        h_ref[...] = h
        gate_ref[...] = _dot(h, wgate_ref[...])

    o_ref[...] = (_dot(h_ref[...], w_ref[...]) * cs_ref[...]).astype(o_ref.dtype)


def _qkvo(x2d, g, w, colscale, wgate, *, tm=1024, tn=1024):
    M, D = x2d.shape
    N = w.shape[1]
    return pl.pallas_call(
        _qkvo_kernel,
        grid=(M // tm, N // tn),
        in_specs=[
            pl.BlockSpec((tm, D), lambda m, j: (m, 0)),
            pl.BlockSpec((1, D), lambda m, j: (0, 0)),
            pl.BlockSpec((D, tn), lambda m, j: (0, j)),
            pl.BlockSpec((1, tn), lambda m, j: (0, j)),
            pl.BlockSpec((D, LANES), lambda m, j: (0, 0)),
        ],
        out_specs=[
            pl.BlockSpec((tm, tn), lambda m, j: (m, j)),
            pl.BlockSpec((tm, LANES), lambda m, j: (m, 0)),
        ],
        out_shape=[
            jax.ShapeDtypeStruct((M, N), _BF16),
            jax.ShapeDtypeStruct((M, LANES), _F32),
        ],
        scratch_shapes=[pltpu.VMEM((tm, D), _BF16)],
        compiler_params=pltpu.CompilerParams(
            dimension_semantics=("arbitrary", "arbitrary"), vmem_limit_bytes=VMEM_LIMIT),
        name="qkvo",
    )(x2d, g, w, colscale, wgate)


def _exact_cumsum_rows(tril, a):
    hi = a.astype(_BF16)
    r1 = a - hi.astype(_F32)
    mid = r1.astype(_BF16)
    lo = (r1 - mid.astype(_F32)).astype(_BF16)
    return _dot(tril, hi) + _dot(tril, mid) + _dot(tril, lo)


def _mlstm_kernel(q_ref, k_ref, v_ref, o_ref, gate_ref, bias_ref, gn_ref, y_ref,
                  c_ref, n_ref, m_ref, *, dqk, dv):
    L = q_ref.shape[0]

    @pl.when(pl.program_id(1) == 0)
    def _():
        c_ref[...] = jnp.zeros(c_ref.shape, _F32)
        n_ref[...] = jnp.zeros(n_ref.shape, _F32)
        m_ref[...] = jnp.zeros(m_ref.shape, _F32)

    gates = gate_ref[...] + bias_ref[...]
    logf = jnp.minimum(gates, 0.0) - jnp.log1p(jnp.exp(-jnp.abs(gates)))
    row = lax.broadcasted_iota(jnp.int32, (L, L), 0)
    col = lax.broadcasted_iota(jnp.int32, (L, L), 1)
    causal = col <= row
    bcum = _exact_cumsum_rows(causal.astype(_BF16), logf)
    gates_t = gates.T
    bcum_t = bcum.T

    for h in range(N_HEADS):
        b_col = bcum[:, N_HEADS + h:N_HEADS + h + 1]
        li_col = gates[:, h:h + 1]
        b_row = bcum_t[N_HEADS + h:N_HEADS + h + 1, :]
        li_row = gates_t[h:h + 1, :]
        m_prev = m_ref[h:h + 1, 0:1]

        dlog = jnp.where(causal, b_col - b_row + li_row, -jnp.inf)
        inter = b_col + m_prev
        m_t = jnp.maximum(inter, jnp.max(dlog, axis=-1, keepdims=True))
        dw = jnp.exp(dlog - m_t)
        a_inter = jnp.exp(inter - m_t)

        q = q_ref[:, h * dqk:(h + 1) * dqk]
        k = k_ref[:, h * dqk:(h + 1) * dqk]
        v = v_ref[:, h * dv:(h + 1) * dv]
        c_prev = c_ref[h]
        n_prev = n_ref[h:h + 1, :]

        s = lax.dot_general(q, k, (((1,), (1,)), ((), ())),
                            preferred_element_type=_F32) * dw
        num = a_inter * _dot(q, c_prev.astype(_BF16)) + _dot(s.astype(_BF16), v)
        qn = jnp.sum(q.astype(_F32) * n_prev, axis=-1, keepdims=True)
        den = a_inter * qn + jnp.sum(s, axis=-1, keepdims=True)
        den = jnp.maximum(jnp.abs(den), jnp.exp(-m_t))
        hm = num / den
        hm = hm * lax.rsqrt(jnp.mean(hm * hm, axis=-1, keepdims=True) + EPS)
        hm = hm * gn_ref[:, h * dv:(h + 1) * dv]
        og = jax.nn.sigmoid(o_ref[:, h * dv:(h + 1) * dv].astype(_F32))
        y_ref[:, h * dv:(h + 1) * dv] = (og * hm).astype(y_ref.dtype)

        b_tot = b_col[L - 1:L, :]
        m_new = m_t[L - 1:L, :]
        a_prev = a_inter[L - 1:L, :]
        w_col = jnp.exp(b_tot - b_col + li_col - m_new)
        kw = k.astype(_F32) * w_col
        c_ref[h] = a_prev * c_prev + _dot(kw.T.astype(_BF16), v)
        n_ref[h:h + 1, :] = a_prev * n_prev + jnp.sum(kw, axis=0, keepdims=True)
        m_ref[h:h + 1, :] = jnp.broadcast_to(m_new, (1, LANES))


def _mlstm(qkvo, gates, gate_bias, gnorm, *, batch, seq, dqk, dv):
    M = qkvo.shape[0]
    L = MLSTM_CHUNK
    nc = seq // L
    dq_all = N_HEADS * dqk
    dv_all = N_HEADS * dv
    assert dv_all == 2 * dq_all
    tok = lambda b, c: b * nc + c
    return pl.pallas_call(
        functools.partial(_mlstm_kernel, dqk=dqk, dv=dv),
        grid=(batch, nc),
        in_specs=[
            pl.BlockSpec((L, dq_all), lambda b, c: (tok(b, c), 0)),
            pl.BlockSpec((L, dq_all), lambda b, c: (tok(b, c), 1)),
            pl.BlockSpec((L, dv_all), lambda b, c: (tok(b, c), 1)),
            pl.BlockSpec((L, dv_all), lambda b, c: (tok(b, c), 2)),
            pl.BlockSpec((L, LANES), lambda b, c: (tok(b, c), 0)),
            pl.BlockSpec((1, LANES), lambda b, c: (0, 0)),
            pl.BlockSpec((1, dv_all), lambda b, c: (0, 0)),
        ],
        out_specs=pl.BlockSpec((L, dv_all), lambda b, c: (tok(b, c), 0)),
        out_shape=jax.ShapeDtypeStruct((M, dv_all), _BF16),
        scratch_shapes=[
            pltpu.VMEM((N_HEADS, dqk, dv), _F32),
            pltpu.VMEM((SUBLANES, dqk), _F32),
            pltpu.VMEM((SUBLANES, LANES), _F32),
        ],
        compiler_params=pltpu.CompilerParams(
            dimension_semantics=("arbitrary", "arbitrary"), vmem_limit_bytes=VMEM_LIMIT),
        name="mlstm",
    )(qkvo, qkvo, qkvo, qkvo, gates, gate_bias, gnorm)


def _out_proj_kernel(x_ref, yc_ref, ym_ref, wc_ref, wm_ref, o_ref):
    acc = _dot(yc_ref[...], wc_ref[...]) + _dot(ym_ref[...], wm_ref[...])
    o_ref[...] = x_ref[...] + acc


def _out_proj(x2d, yc, ym, w_out, *, tm=1024, tn=1024):
    M, D = x2d.shape
    kc = yc.shape[1]
    km = ym.shape[1]
    assert kc == km
    return pl.pallas_call(
        _out_proj_kernel,
        grid=(M // tm, D // tn),
        in_specs=[
            pl.BlockSpec((tm, tn), lambda m, j: (m, j)),
            pl.BlockSpec((tm, kc), lambda m, j: (m, 0)),
            pl.BlockSpec((tm, km), lambda m, j: (m, 0)),
            pl.BlockSpec((kc, tn), lambda m, j: (0, j)),
            pl.BlockSpec((km, tn), lambda m, j: (1, j)),
        ],
        out_specs=pl.BlockSpec((tm, tn), lambda m, j: (m, j)),
        out_shape=jax.ShapeDtypeStruct((M, D), _F32),
        compiler_params=pltpu.CompilerParams(
            dimension_semantics=("arbitrary", "arbitrary"), vmem_limit_bytes=VMEM_LIMIT),
        name="out_proj",
    )(x2d, yc, ym, w_out, w_out)


def _ffn_kernel(x_ref, g_ref, wg_ref, wu_ref, cwg_ref, cwu_ref, bg_ref, bu_ref, wd_ref,
                o_ref, h_ref, halog_ref, halou_ref, *, tiles_per_seq, rows):
    m = pl.program_id(0)
    j = pl.program_id(1)
    tm = x_ref.shape[0]

    @pl.when(j == 0)
    def _():
        x = x_ref[...]
        h_ref[...] = _rmsnorm_rows(x, g_ref[...]).astype(_BF16)
        o_ref[...] = x

    first = m % tiles_per_seq == 0
    prev_g = _load_halo(halog_ref, j, first)
    prev_u = _load_halo(halou_ref, j, first)

    def up_proj(r):
        h = h_ref[r * rows:(r + 1) * rows, :]
        return _dot(h, wg_ref[...]), _dot(h, wu_ref[...])

    n_chunks = tm // rows
    cur = up_proj(0)
    for r in range(n_chunks):
        nxt = up_proj(r + 1) if r + 1 < n_chunks else None
        zg, zu = cur
        gate = _causal_conv3(zg, prev_g, cwg_ref[...]) + bg_ref[...]
        up = _causal_conv3(zu, prev_u, cwu_ref[...]) + bu_ref[...]
        act = (jax.nn.silu(gate) * up).astype(_BF16)
        o_ref[r * rows:(r + 1) * rows, :] += _dot(act, wd_ref[...])
        prev_g = zg[rows - SUBLANES:, :]
        prev_u = zu[rows - SUBLANES:, :]
        cur = nxt
    halog_ref[j] = prev_g
    halou_ref[j] = prev_u


def _ffn(x2d, g, w_up, cw, cb, w_down, *, seq, tm=1024, tf=512, rows=256):
    M, D = x2d.shape
    F = w_down.shape[0]
    nj = F // tf
    return pl.pallas_call(
        functools.partial(_ffn_kernel, tiles_per_seq=seq // tm, rows=rows),
        grid=(M // tm, nj),
        in_specs=[
            pl.BlockSpec((tm, D), lambda m, j: (m, 0)),
            pl.BlockSpec((1, D), lambda m, j: (0, 0)),
            pl.BlockSpec((D, tf), lambda m, j: (0, j)),
            pl.BlockSpec((D, tf), lambda m, j: (0, nj + j)),
            pl.BlockSpec((CONV_WIDTH, tf), lambda m, j: (0, j)),
            pl.BlockSpec((CONV_WIDTH, tf), lambda m, j: (0, nj + j)),
            pl.BlockSpec((1, tf), lambda m, j: (0, j)),
            pl.BlockSpec((1, tf), lambda m, j: (0, nj + j)),
            pl.BlockSpec((tf, D), lambda m, j: (j, 0)),
        ],
        out_specs=pl.BlockSpec((tm, D), lambda m, j: (m, 0)),
        out_shape=jax.ShapeDtypeStruct((M, D), _F32),
        scratch_shapes=[
            pltpu.VMEM((tm, D), _BF16),
            pltpu.VMEM((nj, SUBLANES, tf), _F32),
            pltpu.VMEM((nj, SUBLANES, tf), _F32),
        ],
        compiler_params=pltpu.CompilerParams(
            dimension_semantics=("arbitrary", "arbitrary"), vmem_limit_bytes=VMEM_LIMIT),
        name="ffn",
    )(x2d, g, w_up, w_up, cw, cw, cb, cb, w_down)


def _ple_kernel(x_ref, g_ref, p_ref, wg_ref, wp_ref, gf_ref, o_ref, *, final_norm, rows):
    def normed(r):
        return _rmsnorm_rows(x_ref[r * rows:(r + 1) * rows, :], g_ref[...]).astype(_BF16)

    n_chunks = x_ref.shape[0] // rows
    h = normed(0)
    for r in range(n_chunks):
        h_next = normed(r + 1) if r + 1 < n_chunks else None
        sl = slice(r * rows, (r + 1) * rows)
        gate = jax.nn.sigmoid(_dot(h, wg_ref[...]))
        x = x_ref[sl, :] + gate * _dot(p_ref[sl, :].astype(_BF16), wp_ref[...])
        if final_norm:
            x = _rmsnorm_rows(x, gf_ref[...])
        o_ref[sl, :] = x
        h = h_next


def _ple(x2d, g, p2d, w_gate, w_proj, g_final, *, final_norm, tm=512, rows=256):
    M, D = x2d.shape
    P = p2d.shape[1]
    return pl.pallas_call(
        functools.partial(_ple_kernel, final_norm=final_norm, rows=rows),
        grid=(M // tm,),
        in_specs=[
            pl.BlockSpec((tm, D), lambda m: (m, 0)),
            pl.BlockSpec((1, D), lambda m: (0, 0)),
            pl.BlockSpec((tm, P), lambda m: (m, 0)),
            pl.BlockSpec((D, D), lambda m: (0, 0)),
            pl.BlockSpec((P, D), lambda m: (0, 0)),
            pl.BlockSpec((1, D), lambda m: (0, 0)),
        ],
        out_specs=pl.BlockSpec((tm, D), lambda m: (m, 0)),
        out_shape=jax.ShapeDtypeStruct((M, D), _F32),
        compiler_params=pltpu.CompilerParams(
            dimension_semantics=("arbitrary",), vmem_limit_bytes=VMEM_LIMIT),
        name="ple",
    )(x2d, g, p2d, w_gate, w_proj, g_final)


def kernel(x, p, norm_mix_g, w_in, b_igate, b_fgate, short_conv_w, mh_norm_g, w_out,
           norm_ffn_g, w_up, ffn_conv_w, ffn_conv_b, w_down, norm_ple_g, w_ple_gate,
           w_ple_proj, final_norm_g):
    B, S, D = x.shape
    depth = w_in.shape[0]
    d_conv = short_conv_w.shape[-1]
    d_mlstm = mh_norm_g.shape[-1]
    dv = d_mlstm // N_HEADS
    d_qk = (w_in.shape[-1] - 3 * d_conv - 2 * d_mlstm - 2 * N_HEADS) // 2
    dqk = d_qk // N_HEADS
    M = B * S

    x2d = x.reshape(M, D)
    row = lambda a: a.reshape(1, -1).astype(_F32)
    colscale = jnp.concatenate([
        jnp.full((1, d_qk), dqk ** -0.5, _F32),
        jnp.ones((1, d_qk + 2 * d_mlstm), _F32)], axis=1)

    for i in range(depth):
        w = w_in[i]
        c0 = 3 * d_conv
        c1 = c0 + 2 * d_qk + 2 * d_mlstm
        wb = w[:, 0:d_conv].astype(_BF16)
        wc = w[:, d_conv:2 * d_conv].astype(_BF16)
        wu = w[:, 2 * d_conv:c0].astype(_BF16)
        w_qkvo = w[:, c0:c1].astype(_BF16)
        w_gates = jnp.pad(w[:, c1:], ((0, 0), (0, LANES - 2 * N_HEADS))).astype(_BF16)
        gate_bias = jnp.pad(jnp.concatenate([b_igate[i], b_fgate[i]]).astype(_F32),
                            (0, LANES - 2 * N_HEADS)).reshape(1, LANES)

        g_mix = row(norm_mix_g[i])
        y_conv = _conv_group(x2d, g_mix, wb, wc, wu, short_conv_w[i].astype(_F32), seq=S)
        qkvo, gates = _qkvo(x2d, g_mix, w_qkvo, colscale, w_gates)
        y_m = _mlstm(qkvo, gates, gate_bias, row(mh_norm_g[i]),
                     batch=B, seq=S, dqk=dqk, dv=dv)
        x2d = _out_proj(x2d, y_conv, y_m, w_out[i].astype(_BF16))
        x2d = _ffn(x2d, row(norm_ffn_g[i]), w_up[i].astype(_BF16),
                   ffn_conv_w[i].astype(_F32), row(ffn_conv_b[i]),
                   w_down[i].astype(_BF16), seq=S)
        x2d = _ple(x2d, row(norm_ple_g[i]), p[i].reshape(M, -1), w_ple_gate[i].astype(_BF16),
                   w_ple_proj[i].astype(_BF16), row(final_norm_g),
                   final_norm=(i == depth - 1))
    return x2d.reshape(B, S, D)
```

```python
import functools

import jax
import jax.numpy as jnp
from jax import lax
from jax.experimental import pallas as pl
from jax.experimental.pallas import tpu as pltpu

N_HEADS = 4
CONV_WIDTH = 3
EPS = 1e-6
LANES = 128
SUBLANES = 8
MLSTM_CHUNK = 256
W_COLS = 512
VMEM_LIMIT = 56 * 1024 * 1024

_BF16 = jnp.bfloat16
_F32 = jnp.float32


def _dot(a, b):
    return jnp.dot(a, b, preferred_element_type=_F32)


def _col_blocks(w):
    k, n = w.shape
    return w.astype(_BF16).reshape(k, n // W_COLS, W_COLS).transpose(1, 0, 2)


def _rmsnorm_rows(x, g):
    ms = jnp.mean(x * x, axis=-1, keepdims=True)
    return x * lax.rsqrt(ms + EPS) * g


def _causal_conv3(z, prev, cw):
    n = z.shape[0]
    ext = jnp.concatenate([prev, z], axis=0)
    z2 = ext[SUBLANES - 2:SUBLANES - 2 + n, :]
    z1 = ext[SUBLANES - 1:SUBLANES - 1 + n, :]
    return z2 * cw[0:1, :] + z1 * cw[1:2, :] + z * cw[2:3, :]


def _load_halo(halo_ref, j, first_tile):
    @pl.when(first_tile)
    def _():
        halo_ref[j] = jnp.zeros(halo_ref.shape[1:], _F32)

    return halo_ref[j]


def _cast_specs(weights, n_steps, step_index):
    specs = []
    for w in weights:
        k, n = w.shape
        slab = k // n_steps
        assert slab * n_steps == k and slab % (2 * SUBLANES) == 0
        specs.append(pl.BlockSpec((slab, n), lambda *idx: (step_index(*idx), 0)))
    return specs


def _cast_slabs(refs):
    half = len(refs) // 2
    for src, dst in zip(refs[:half], refs[half:]):
        dst[...] = src[...].astype(dst.dtype)


def _conv_group_kernel(x_ref, g_ref, wb_ref, wc_ref, wu_ref, cw_ref, *rest,
                       tiles_per_seq, rows, n_cast):
    cast_in, (o_ref, *cast_out) = rest[:n_cast], rest[n_cast:2 * n_cast + 1]
    h_ref, halo_ref = rest[2 * n_cast + 1:]
    m = pl.program_id(0)
    j = pl.program_id(1)
    tm = x_ref.shape[0]

    _cast_slabs(list(cast_in) + list(cast_out))

    @pl.when(j == 0)
    def _():
        h_ref[...] = _rmsnorm_rows(x_ref[...], g_ref[...]).astype(_BF16)

    def proj(r):
        h = h_ref[r * rows:(r + 1) * rows, :]
        return _dot(h, wb_ref[...]), _dot(h, wc_ref[...]), _dot(h, wu_ref[...])

    prev = _load_halo(halo_ref, j, m % tiles_per_seq == 0)
    n_chunks = tm // rows
    cur = proj(0)
    for r in range(n_chunks):
        nxt = proj(r + 1) if r + 1 < n_chunks else None
        gb, gc, u = cur
        z = gc * u
        y = _causal_conv3(z, prev, cw_ref[...])
        o_ref[r * rows:(r + 1) * rows, :] = (gb * y).astype(o_ref.dtype)
        prev = z[rows - SUBLANES:, :]
        cur = nxt
    halo_ref[j] = prev


def _conv_group(x2d, g, w_in, cw, cast_weights, *, seq, tm=1024, tc=512, rows=256):
    M, D = x2d.shape
    C = cw.shape[1]
    nj = C // tc
    n_steps = (M // tm) * nj
    cast_specs = _cast_specs(cast_weights, n_steps, lambda m, j: m * nj + j)
    outs = pl.pallas_call(
        functools.partial(_conv_group_kernel, tiles_per_seq=seq // tm, rows=rows,
                          n_cast=len(cast_weights)),
        grid=(M // tm, nj),
        in_specs=[
            pl.BlockSpec((tm, D), lambda m, j: (m, 0)),
            pl.BlockSpec((1, D), lambda m, j: (0, 0)),
            pl.BlockSpec((D, tc), lambda m, j: (0, j)),
            pl.BlockSpec((D, tc), lambda m, j: (0, nj + j)),
            pl.BlockSpec((D, tc), lambda m, j: (0, 2 * nj + j)),
            pl.BlockSpec((CONV_WIDTH, tc), lambda m, j: (0, j)),
        ] + cast_specs,
        out_specs=[pl.BlockSpec((tm, tc), lambda m, j: (m, j))] + cast_specs,
        out_shape=[jax.ShapeDtypeStruct((M, C), _BF16)]
        + [jax.ShapeDtypeStruct(w.shape, _BF16) for w in cast_weights],
        scratch_shapes=[
            pltpu.VMEM((tm, D), _BF16),
            pltpu.VMEM((nj, SUBLANES, tc), _F32),
        ],
        compiler_params=pltpu.CompilerParams(
            dimension_semantics=("arbitrary", "arbitrary"), vmem_limit_bytes=VMEM_LIMIT),
        name="conv_group",
    )(x2d, g, w_in, w_in, w_in, cw, *cast_weights)
    return outs


def _qkvo_kernel(x_ref, g_ref, w_ref, cs_ref, wgate_ref, o_ref, gate_ref, h_ref):
    j = pl.program_id(1)

    @pl.when(j == 0)
    def _():
        h = _rmsnorm_rows(x_ref[...], g_ref[...]).astype(_BF16)
        h_ref[...] = h
        gate_ref[...] = _dot(h, wgate_ref[...])

    o_ref[...] = (_dot(h_ref[...], w_ref[...]) * cs_ref[...]).astype(o_ref.dtype)


def _qkvo(x2d, g, w_in, col0, colscale, wgate, *, tm=1024, tn=1024):
    M, D = x2d.shape
    N = colscale.shape[1]
    assert col0 % tn == 0 and N % tn == 0
    jb = col0 // tn
    return pl.pallas_call(
        _qkvo_kernel,
        grid=(M // tm, N // tn),
        in_specs=[
            pl.BlockSpec((tm, D), lambda m, j: (m, 0)),
            pl.BlockSpec((1, D), lambda m, j: (0, 0)),
            pl.BlockSpec((D, tn), lambda m, j: (0, jb + j)),
            pl.BlockSpec((1, tn), lambda m, j: (0, j)),
            pl.BlockSpec((D, LANES), lambda m, j: (0, 0)),
        ],
        out_specs=[
            pl.BlockSpec((tm, tn), lambda m, j: (m, j)),
            pl.BlockSpec((tm, LANES), lambda m, j: (m, 0)),
        ],
        out_shape=[
            jax.ShapeDtypeStruct((M, N), _BF16),
            jax.ShapeDtypeStruct((M, LANES), _F32),
        ],
        scratch_shapes=[pltpu.VMEM((tm, D), _BF16)],
        compiler_params=pltpu.CompilerParams(
            dimension_semantics=("arbitrary", "arbitrary"), vmem_limit_bytes=VMEM_LIMIT),
        name="qkvo",
    )(x2d, g, w_in, colscale, wgate)


def _exact_cumsum_rows(tril, a):
    hi = a.astype(_BF16)
    r1 = a - hi.astype(_F32)
    mid = r1.astype(_BF16)
    lo = (r1 - mid.astype(_F32)).astype(_BF16)
    return _dot(tril, hi) + _dot(tril, mid) + _dot(tril, lo)


def _mlstm_kernel(q_ref, k_ref, v_ref, o_ref, gate_ref, bias_ref, gn_ref, *rest,
                  dqk, dv, n_cast):
    cast_in, (y_ref, *cast_out) = rest[:n_cast], rest[n_cast:2 * n_cast + 1]
    c_ref, n_ref, m_ref = rest[2 * n_cast + 1:]
    L = q_ref.shape[0]

    _cast_slabs(list(cast_in) + list(cast_out))

    @pl.when(pl.program_id(1) == 0)
    def _():
        c_ref[...] = jnp.zeros(c_ref.shape, _F32)
        n_ref[...] = jnp.zeros(n_ref.shape, _F32)
        m_ref[...] = jnp.zeros(m_ref.shape, _F32)

    gates = gate_ref[...] + bias_ref[...]
    logf = jnp.minimum(gates, 0.0) - jnp.log1p(jnp.exp(-jnp.abs(gates)))
    row = lax.broadcasted_iota(jnp.int32, (L, L), 0)
    col = lax.broadcasted_iota(jnp.int32, (L, L), 1)
    causal = col <= row
    bcum = _exact_cumsum_rows(causal.astype(_BF16), logf)
    gates_t = gates.T
    bcum_t = bcum.T

    for h in range(N_HEADS):
        b_col = bcum[:, N_HEADS + h:N_HEADS + h + 1]
        li_col = gates[:, h:h + 1]
        b_row = bcum_t[N_HEADS + h:N_HEADS + h + 1, :]
        li_row = gates_t[h:h + 1, :]
        m_prev = m_ref[h:h + 1, 0:1]

        dlog = jnp.where(causal, b_col - b_row + li_row, -jnp.inf)
        inter = b_col + m_prev
        m_t = jnp.maximum(inter, jnp.max(dlog, axis=-1, keepdims=True))
        dw = jnp.exp(dlog - m_t)
        a_inter = jnp.exp(inter - m_t)

        q = q_ref[:, h * dqk:(h + 1) * dqk]
        k = k_ref[:, h * dqk:(h + 1) * dqk]
        v = v_ref[:, h * dv:(h + 1) * dv]
        c_prev = c_ref[h]
        n_prev = n_ref[h:h + 1, :]

        s = lax.dot_general(q, k, (((1,), (1,)), ((), ())),
                            preferred_element_type=_F32) * dw
        num = a_inter * _dot(q, c_prev.astype(_BF16)) + _dot(s.astype(_BF16), v)
        qn = jnp.sum(q.astype(_F32) * n_prev, axis=-1, keepdims=True)
        den = a_inter * qn + jnp.sum(s, axis=-1, keepdims=True)
        den = jnp.maximum(jnp.abs(den), jnp.exp(-m_t))
        hm = num / den
        hm = hm * lax.rsqrt(jnp.mean(hm * hm, axis=-1, keepdims=True) + EPS)
        hm = hm * gn_ref[:, h * dv:(h + 1) * dv]
        og = jax.nn.sigmoid(o_ref[:, h * dv:(h + 1) * dv].astype(_F32))
        y_ref[:, h * dv:(h + 1) * dv] = (og * hm).astype(y_ref.dtype)

        b_tot = b_col[L - 1:L, :]
        m_new = m_t[L - 1:L, :]
        a_prev = a_inter[L - 1:L, :]
        w_col = jnp.exp(b_tot - b_col + li_col - m_new)
        kw = k.astype(_F32) * w_col
        c_ref[h] = a_prev * c_prev + _dot(kw.T.astype(_BF16), v)
        n_ref[h:h + 1, :] = a_prev * n_prev + jnp.sum(kw, axis=0, keepdims=True)
        m_ref[h:h + 1, :] = jnp.broadcast_to(m_new, (1, LANES))


def _mlstm(qkvo, gates, gate_bias, gnorm, cast_weights, *, batch, seq, dqk, dv):
    M = qkvo.shape[0]
    L = MLSTM_CHUNK
    nc = seq // L
    dq_all = N_HEADS * dqk
    dv_all = N_HEADS * dv
    assert dv_all == 2 * dq_all
    tok = lambda b, c: b * nc + c
    cast_specs = _cast_specs(cast_weights, batch * nc, tok)
    return pl.pallas_call(
        functools.partial(_mlstm_kernel, dqk=dqk, dv=dv, n_cast=len(cast_weights)),
        grid=(batch, nc),
        in_specs=[
            pl.BlockSpec((L, dq_all), lambda b, c: (tok(b, c), 0)),
            pl.BlockSpec((L, dq_all), lambda b, c: (tok(b, c), 1)),
            pl.BlockSpec((L, dv_all), lambda b, c: (tok(b, c), 1)),
            pl.BlockSpec((L, dv_all), lambda b, c: (tok(b, c), 2)),
            pl.BlockSpec((L, LANES), lambda b, c: (tok(b, c), 0)),
            pl.BlockSpec((1, LANES), lambda b, c: (0, 0)),
            pl.BlockSpec((1, dv_all), lambda b, c: (0, 0)),
        ] + cast_specs,
        out_specs=[pl.BlockSpec((L, dv_all), lambda b, c: (tok(b, c), 0))] + cast_specs,
        out_shape=[jax.ShapeDtypeStruct((M, dv_all), _BF16)]
        + [jax.ShapeDtypeStruct(w.shape, _BF16) for w in cast_weights],
        scratch_shapes=[
            pltpu.VMEM((N_HEADS, dqk, dv), _F32),
            pltpu.VMEM((SUBLANES, dqk), _F32),
            pltpu.VMEM((SUBLANES, LANES), _F32),
        ],
        compiler_params=pltpu.CompilerParams(
            dimension_semantics=("arbitrary", "arbitrary"), vmem_limit_bytes=VMEM_LIMIT),
        name="mlstm",
    )(qkvo, qkvo, qkvo, qkvo, gates, gate_bias, gnorm, *cast_weights)


def _out_proj_kernel(x_ref, yc_ref, ym_ref, wc_ref, wm_ref, o_ref):
    acc = _dot(yc_ref[...], wc_ref[...]) + _dot(ym_ref[...], wm_ref[...])
    o_ref[...] = x_ref[...] + acc


def _out_proj(x2d, yc, ym, w_out, *, tm=1024, tn=1024):
    M, D = x2d.shape
    kc = yc.shape[1]
    km = ym.shape[1]
    assert kc == km
    return pl.pallas_call(
        _out_proj_kernel,
        grid=(M // tm, D // tn),
        in_specs=[
            pl.BlockSpec((tm, tn), lambda m, j: (m, j)),
            pl.BlockSpec((tm, kc), lambda m, j: (m, 0)),
            pl.BlockSpec((tm, km), lambda m, j: (m, 0)),
            pl.BlockSpec((kc, tn), lambda m, j: (0, j)),
            pl.BlockSpec((km, tn), lambda m, j: (1, j)),
        ],
        out_specs=pl.BlockSpec((tm, tn), lambda m, j: (m, j)),
        out_shape=jax.ShapeDtypeStruct((M, D), _F32),
        compiler_params=pltpu.CompilerParams(
            dimension_semantics=("arbitrary", "arbitrary"), vmem_limit_bytes=VMEM_LIMIT),
        name="out_proj",
    )(x2d, yc, ym, w_out, w_out)


def _ffn_kernel(x_ref, g_ref, wg_ref, wu_ref, cwg_ref, cwu_ref, bg_ref, bu_ref, wd_ref,
                o_ref, h_ref, halog_ref, halou_ref, *, tiles_per_seq, rows):
    m = pl.program_id(0)
    j = pl.program_id(1)
    tm = x_ref.shape[0]

    @pl.when(j == 0)
    def _():
        x = x_ref[...]
        h_ref[...] = _rmsnorm_rows(x, g_ref[...]).astype(_BF16)
        o_ref[...] = x

    first = m % tiles_per_seq == 0
    prev_g = _load_halo(halog_ref, j, first)
    prev_u = _load_halo(halou_ref, j, first)

    def up_proj(r):
        h = h_ref[r * rows:(r + 1) * rows, :]
        return _dot(h, wg_ref[...]), _dot(h, wu_ref[...])

    n_chunks = tm // rows
    cur = up_proj(0)
    for r in range(n_chunks):
        nxt = up_proj(r + 1) if r + 1 < n_chunks else None
        zg, zu = cur
        gate = _causal_conv3(zg, prev_g, cwg_ref[...]) + bg_ref[...]
        up = _causal_conv3(zu, prev_u, cwu_ref[...]) + bu_ref[...]
        act = (jax.nn.silu(gate) * up).astype(_BF16)
        o_ref[r * rows:(r + 1) * rows, :] += _dot(act, wd_ref[...])
        prev_g = zg[rows - SUBLANES:, :]
        prev_u = zu[rows - SUBLANES:, :]
        cur = nxt
    halog_ref[j] = prev_g
    halou_ref[j] = prev_u


def _ffn(x2d, g, w_up, cw, cb, w_down, *, seq, tm=1024, tf=512, rows=256):
    M, D = x2d.shape
    F = w_down.shape[0]
    nj = F // tf
    return pl.pallas_call(
        functools.partial(_ffn_kernel, tiles_per_seq=seq // tm, rows=rows),
        grid=(M // tm, nj),
        in_specs=[
            pl.BlockSpec((tm, D), lambda m, j: (m, 0)),
            pl.BlockSpec((1, D), lambda m, j: (0, 0)),
            pl.BlockSpec((D, tf), lambda m, j: (0, j)),
            pl.BlockSpec((D, tf), lambda m, j: (0, nj + j)),
            pl.BlockSpec((CONV_WIDTH, tf), lambda m, j: (0, j)),
            pl.BlockSpec((CONV_WIDTH, tf), lambda m, j: (0, nj + j)),
            pl.BlockSpec((1, tf), lambda m, j: (0, j)),
            pl.BlockSpec((1, tf), lambda m, j: (0, nj + j)),
            pl.BlockSpec((tf, D), lambda m, j: (j, 0)),
        ],
        out_specs=pl.BlockSpec((tm, D), lambda m, j: (m, 0)),
        out_shape=jax.ShapeDtypeStruct((M, D), _F32),
        scratch_shapes=[
            pltpu.VMEM((tm, D), _BF16),
            pltpu.VMEM((nj, SUBLANES, tf), _F32),
            pltpu.VMEM((nj, SUBLANES, tf), _F32),
        ],
        compiler_params=pltpu.CompilerParams(
            dimension_semantics=("arbitrary", "arbitrary"), vmem_limit_bytes=VMEM_LIMIT),
        name="ffn",
    )(x2d, g, w_up, w_up, cw, cw, cb, cb, w_down)


def _ple_kernel(x_ref, g_ref, p_ref, wg_ref, wp_ref, gf_ref, o_ref, *, final_norm, rows):
    def normed(r):
        return _rmsnorm_rows(x_ref[r * rows:(r + 1) * rows, :], g_ref[...]).astype(_BF16)

    n_chunks = x_ref.shape[0] // rows
    h = normed(0)
    for r in range(n_chunks):
        h_next = normed(r + 1) if r + 1 < n_chunks else None
        sl = slice(r * rows, (r + 1) * rows)
        gate = jax.nn.sigmoid(_dot(h, wg_ref[...]))
        x = x_ref[sl, :] + gate * _dot(p_ref[sl, :].astype(_BF16), wp_ref[...])
        if final_norm:
            x = _rmsnorm_rows(x, gf_ref[...])
        o_ref[sl, :] = x
        h = h_next


def _ple(x2d, g, p2d, w_gate, w_proj, g_final, *, final_norm, tm=512, rows=256):
    M, D = x2d.shape
    P = p2d.shape[1]
    return pl.pallas_call(
        functools.partial(_ple_kernel, final_norm=final_norm, rows=rows),
        grid=(M // tm,),
        in_specs=[
            pl.BlockSpec((tm, D), lambda m: (m, 0)),
            pl.BlockSpec((1, D), lambda m: (0, 0)),
            pl.BlockSpec((tm, P), lambda m: (m, 0)),
            pl.BlockSpec((D, D), lambda m: (0, 0)),
            pl.BlockSpec((P, D), lambda m: (0, 0)),
            pl.BlockSpec((1, D), lambda m: (0, 0)),
        ],
        out_specs=pl.BlockSpec((tm, D), lambda m: (m, 0)),
        out_shape=jax.ShapeDtypeStruct((M, D), _F32),
        compiler_params=pltpu.CompilerParams(
            dimension_semantics=("arbitrary",), vmem_limit_bytes=VMEM_LIMIT),
        name="ple",
    )(x2d, g, p2d, w_gate, w_proj, g_final)


def kernel(x, p, norm_mix_g, w_in, b_igate, b_fgate, short_conv_w, mh_norm_g, w_out,
           norm_ffn_g, w_up, ffn_conv_w, ffn_conv_b, w_down, norm_ple_g, w_ple_gate,
           w_ple_proj, final_norm_g):
    B, S, D = x.shape
    depth = w_in.shape[0]
    d_conv = short_conv_w.shape[-1]
    d_mlstm = mh_norm_g.shape[-1]
    dv = d_mlstm // N_HEADS
    d_qk = (w_in.shape[-1] - 3 * d_conv - 2 * d_mlstm - 2 * N_HEADS) // 2
    dqk = d_qk // N_HEADS
    M = B * S

    x2d = x.reshape(M, D)
    row = lambda a: a.reshape(1, -1).astype(_F32)
    colscale = jnp.concatenate([
        jnp.full((1, d_qk), dqk ** -0.5, _F32),
        jnp.ones((1, d_qk + 2 * d_mlstm), _F32)], axis=1)

    for i in range(depth):
        w = w_in[i]
        c0 = 3 * d_conv
        c1 = c0 + 2 * d_qk + 2 * d_mlstm
        w_main = w[:, :c1].astype(_BF16)
        w_gates = jnp.pad(w[:, c1:], ((0, 0), (0, LANES - 2 * N_HEADS))).astype(_BF16)
        gate_bias = jnp.pad(jnp.concatenate([b_igate[i], b_fgate[i]]).astype(_F32),
                            (0, LANES - 2 * N_HEADS)).reshape(1, LANES)

        g_mix = row(norm_mix_g[i])
        y_conv, w_out_b, w_pg_b, w_pp_b = _conv_group(
            x2d, g_mix, w_main, short_conv_w[i].astype(_F32),
            [w_out[i], w_ple_gate[i], w_ple_proj[i]], seq=S)
        qkvo, gates = _qkvo(x2d, g_mix, w_main, c0, colscale, w_gates)
        y_m, w_up_b, w_down_b = _mlstm(qkvo, gates, gate_bias, row(mh_norm_g[i]),
                                       [w_up[i], w_down[i]], batch=B, seq=S, dqk=dqk, dv=dv)
        x2d = _out_proj(x2d, y_conv, y_m, w_out_b)
        x2d = _ffn(x2d, row(norm_ffn_g[i]), w_up_b, ffn_conv_w[i].astype(_F32),
                   row(ffn_conv_b[i]), w_down_b, seq=S)
        x2d = _ple(x2d, row(norm_ple_g[i]), p[i].reshape(M, -1), w_pg_b, w_pp_b,
                   row(final_norm_g), final_norm=(i == depth - 1))
    return x2d.reshape(B, S, D)
```

```python
import functools

import jax
import jax.numpy as jnp
from jax import lax
from jax.experimental import pallas as pl
from jax.experimental.pallas import tpu as pltpu

N_HEADS = 4
CONV_WIDTH = 3
EPS = 1e-6
LANES = 128
SUBLANES = 8
MLSTM_CHUNK = 256
W_COLS = 512
VMEM_LIMIT = 62 * 1024 * 1024

_BF16 = jnp.bfloat16
_F32 = jnp.float32


def _dot(a, b):
    return jnp.dot(a, b, preferred_element_type=_F32)


def _dot_nt(a, bt):
    return lax.dot_general(a, bt, (((1,), (1,)), ((), ())), preferred_element_type=_F32)


def _rmsnorm_rows(x, g):
    ms = jnp.mean(x * x, axis=-1, keepdims=True)
    return x * lax.rsqrt(ms + EPS) * g


def _causal_conv3(z, prev, cw):
    n = z.shape[0]
    ext = jnp.concatenate([prev, z], axis=0)
    z2 = ext[SUBLANES - 2:SUBLANES - 2 + n, :]
    z1 = ext[SUBLANES - 1:SUBLANES - 1 + n, :]
    return z2 * cw[0:1, :] + z1 * cw[1:2, :] + z * cw[2:3, :]


def _load_halo(halo_ref, j, first_tile):
    @pl.when(first_tile)
    def _():
        halo_ref[j] = jnp.zeros(halo_ref.shape[1:], _F32)

    return halo_ref[j]


def _cast_plan(items, n_steps, step_index):
    in_specs, out_specs, shapes = [], [], []
    for a, row0, n_rows in items:
        slab = n_rows // n_steps
        assert slab * n_steps == n_rows and slab % (2 * SUBLANES) == 0 and row0 % slab == 0
        cols = a.shape[1]
        in_specs.append(pl.BlockSpec(
            (slab, cols), lambda *idx, b0=row0 // slab: (b0 + step_index(*idx), 0)))
        out_specs.append(pl.BlockSpec((slab, cols), lambda *idx: (step_index(*idx), 0)))
        shapes.append(jax.ShapeDtypeStruct((n_rows, cols), _BF16))
    return in_specs, out_specs, shapes


def _cast_slabs(refs):
    half = len(refs) // 2
    for src, dst in zip(refs[:half], refs[half:]):
        dst[...] = src[...].astype(dst.dtype)


def _conv_group_kernel(x_ref, g_ref, wb_ref, wc_ref, wu_ref, cw_ref, *rest,
                       tiles_per_seq, rows, n_cast):
    cast_in, (o_ref, *cast_out) = rest[:n_cast], rest[n_cast:2 * n_cast + 1]
    h_ref, halo_ref = rest[2 * n_cast + 1:]
    m = pl.program_id(0)
    j = pl.program_id(1)
    tm = x_ref.shape[0]

    _cast_slabs(list(cast_in) + list(cast_out))

    @pl.when(j == 0)
    def _():
        h_ref[...] = _rmsnorm_rows(x_ref[...], g_ref[...]).astype(_BF16)

    wb = wb_ref[...].astype(_BF16)
    wc = wc_ref[...].astype(_BF16)
    wu = wu_ref[...].astype(_BF16)

    def proj(r):
        h = h_ref[r * rows:(r + 1) * rows, :]
        return _dot_nt(h, wb), _dot_nt(h, wc), _dot_nt(h, wu)

    prev = _load_halo(halo_ref, j, m % tiles_per_seq == 0)
    n_chunks = tm // rows
    cur = proj(0)
    for r in range(n_chunks):
        nxt = proj(r + 1) if r + 1 < n_chunks else None
        gb, gc, u = cur
        z = gc * u
        y = _causal_conv3(z, prev, cw_ref[...])
        o_ref[r * rows:(r + 1) * rows, :] = (gb * y).astype(o_ref.dtype)
        prev = z[rows - SUBLANES:, :]
        cur = nxt
    halo_ref[j] = prev


def _conv_group(x2d, g, w_in_t, cw, cast_items, *, seq, tm=1024, tc=256, rows=256):
    M, D = x2d.shape
    C = cw.shape[1]
    nj = C // tc
    n_steps = (M // tm) * nj
    cast_in, cast_out, cast_shapes = _cast_plan(cast_items, n_steps, lambda m, j: m * nj + j)
    return pl.pallas_call(
        functools.partial(_conv_group_kernel, tiles_per_seq=seq // tm, rows=rows,
                          n_cast=len(cast_items)),
        grid=(M // tm, nj),
        in_specs=[
            pl.BlockSpec((tm, D), lambda m, j: (m, 0)),
            pl.BlockSpec((1, D), lambda m, j: (0, 0)),
            pl.BlockSpec((tc, D), lambda m, j: (j, 0)),
            pl.BlockSpec((tc, D), lambda m, j: (nj + j, 0)),
            pl.BlockSpec((tc, D), lambda m, j: (2 * nj + j, 0)),
            pl.BlockSpec((CONV_WIDTH, tc), lambda m, j: (0, j)),
        ] + cast_in,
        out_specs=[pl.BlockSpec((tm, tc), lambda m, j: (m, j))] + cast_out,
        out_shape=[jax.ShapeDtypeStruct((M, C), _BF16)] + cast_shapes,
        scratch_shapes=[
            pltpu.VMEM((tm, D), _BF16),
            pltpu.VMEM((nj, SUBLANES, tc), _F32),
        ],
        compiler_params=pltpu.CompilerParams(
            dimension_semantics=("arbitrary", "arbitrary"), vmem_limit_bytes=VMEM_LIMIT),
        name="conv_group",
    )(x2d, g, w_in_t, w_in_t, w_in_t, cw, *[a for a, _, _ in cast_items])


def _qkvo_kernel(x_ref, g_ref, w_ref, cs_ref, wgate_ref, o_ref, gate_ref, h_ref):
    j = pl.program_id(1)

    @pl.when(j == 0)
    def _():
        h = _rmsnorm_rows(x_ref[...], g_ref[...]).astype(_BF16)
        h_ref[...] = h
        gate_ref[...] = _dot_nt(h, wgate_ref[...])

    o_ref[...] = (_dot_nt(h_ref[...], w_ref[...]) * cs_ref[...]).astype(o_ref.dtype)


def _qkvo(x2d, g, w_in_t, row0, colscale, wgate, *, tm=1024, tn=1024):
    M, D = x2d.shape
    N = colscale.shape[1]
    assert row0 % tn == 0 and N % tn == 0
    jb = row0 // tn
    return pl.pallas_call(
        _qkvo_kernel,
        grid=(M // tm, N // tn),
        in_specs=[
            pl.BlockSpec((tm, D), lambda m, j: (m, 0)),
            pl.BlockSpec((1, D), lambda m, j: (0, 0)),
            pl.BlockSpec((tn, D), lambda m, j: (jb + j, 0)),
            pl.BlockSpec((1, tn), lambda m, j: (0, j)),
            pl.BlockSpec((LANES, D), lambda m, j: (0, 0)),
        ],
        out_specs=[
            pl.BlockSpec((tm, tn), lambda m, j: (m, j)),
            pl.BlockSpec((tm, LANES), lambda m, j: (m, 0)),
        ],
        out_shape=[
            jax.ShapeDtypeStruct((M, N), _BF16),
            jax.ShapeDtypeStruct((M, LANES), _F32),
        ],
        scratch_shapes=[pltpu.VMEM((tm, D), _BF16)],
        compiler_params=pltpu.CompilerParams(
            dimension_semantics=("arbitrary", "arbitrary"), vmem_limit_bytes=VMEM_LIMIT),
        name="qkvo",
    )(x2d, g, w_in_t, colscale, wgate)


def _exact_cumsum_rows(tril, a):
    hi = a.astype(_BF16)
    r1 = a - hi.astype(_F32)
    mid = r1.astype(_BF16)
    lo = (r1 - mid.astype(_F32)).astype(_BF16)
    return _dot(tril, hi) + _dot(tril, mid) + _dot(tril, lo)


def _mlstm_kernel(q_ref, k_ref, v_ref, o_ref, gate_ref, bias_ref, gn_ref, *rest,
                  dqk, dv, n_cast):
    cast_in, (y_ref, *cast_out) = rest[:n_cast], rest[n_cast:2 * n_cast + 1]
    c_ref, n_ref, m_ref = rest[2 * n_cast + 1:]
    L = q_ref.shape[0]

    _cast_slabs(list(cast_in) + list(cast_out))

    @pl.when(pl.program_id(1) == 0)
    def _():
        c_ref[...] = jnp.zeros(c_ref.shape, _F32)
        n_ref[...] = jnp.zeros(n_ref.shape, _F32)
        m_ref[...] = jnp.zeros(m_ref.shape, _F32)

    gates = gate_ref[...] + bias_ref[...]
    logf = jnp.minimum(gates, 0.0) - jnp.log1p(jnp.exp(-jnp.abs(gates)))
    row = lax.broadcasted_iota(jnp.int32, (L, L), 0)
    col = lax.broadcasted_iota(jnp.int32, (L, L), 1)
    causal = col <= row
    bcum = _exact_cumsum_rows(causal.astype(_BF16), logf)
    gates_t = gates.T
    bcum_t = bcum.T

    for h in range(N_HEADS):
        b_col = bcum[:, N_HEADS + h:N_HEADS + h + 1]
        li_col = gates[:, h:h + 1]
        b_row = bcum_t[N_HEADS + h:N_HEADS + h + 1, :]
        li_row = gates_t[h:h + 1, :]
        m_prev = m_ref[h:h + 1, 0:1]

        dlog = jnp.where(causal, b_col - b_row + li_row, -jnp.inf)
        inter = b_col + m_prev
        m_t = jnp.maximum(inter, jnp.max(dlog, axis=-1, keepdims=True))
        dw = jnp.exp(dlog - m_t)
        a_inter = jnp.exp(inter - m_t)

        q = q_ref[:, h * dqk:(h + 1) * dqk]
        k = k_ref[:, h * dqk:(h + 1) * dqk]
        v = v_ref[:, h * dv:(h + 1) * dv]
        c_prev = c_ref[h]
        n_prev = n_ref[h:h + 1, :]

        s = lax.dot_general(q, k, (((1,), (1,)), ((), ())),
                            preferred_element_type=_F32) * dw
        num = a_inter * _dot(q, c_prev.astype(_BF16)) + _dot(s.astype(_BF16), v)
        qn = jnp.sum(q.astype(_F32) * n_prev, axis=-1, keepdims=True)
        den = a_inter * qn + jnp.sum(s, axis=-1, keepdims=True)
        den = jnp.maximum(jnp.abs(den), jnp.exp(-m_t))
        hm = num / den
        hm = hm * lax.rsqrt(jnp.mean(hm * hm, axis=-1, keepdims=True) + EPS)
        hm = hm * gn_ref[:, h * dv:(h + 1) * dv]
        og = jax.nn.sigmoid(o_ref[:, h * dv:(h + 1) * dv].astype(_F32))
        y_ref[:, h * dv:(h + 1) * dv] = (og * hm).astype(y_ref.dtype)

        b_tot = b_col[L - 1:L, :]
        m_new = m_t[L - 1:L, :]
        a_prev = a_inter[L - 1:L, :]
        w_col = jnp.exp(b_tot - b_col + li_col - m_new)
        kw = k.astype(_F32) * w_col
        c_ref[h] = a_prev * c_prev + _dot(kw.T.astype(_BF16), v)
        n_ref[h:h + 1, :] = a_prev * n_prev + jnp.sum(kw, axis=0, keepdims=True)
        m_ref[h:h + 1, :] = jnp.broadcast_to(m_new, (1, LANES))


def _mlstm(qkvo, gates, gate_bias, gnorm, cast_items, *, batch, seq, dqk, dv):
    M = qkvo.shape[0]
    L = MLSTM_CHUNK
    nc = seq // L
    dq_all = N_HEADS * dqk
    dv_all = N_HEADS * dv
    assert dv_all == 2 * dq_all
    tok = lambda b, c: b * nc + c
    cast_in, cast_out, cast_shapes = _cast_plan(cast_items, batch * nc, tok)
    return pl.pallas_call(
        functools.partial(_mlstm_kernel, dqk=dqk, dv=dv, n_cast=len(cast_items)),
        grid=(batch, nc),
        in_specs=[
            pl.BlockSpec((L, dq_all), lambda b, c: (tok(b, c), 0)),
            pl.BlockSpec((L, dq_all), lambda b, c: (tok(b, c), 1)),
            pl.BlockSpec((L, dv_all), lambda b, c: (tok(b, c), 1)),
            pl.BlockSpec((L, dv_all), lambda b, c: (tok(b, c), 2)),
            pl.BlockSpec((L, LANES), lambda b, c: (tok(b, c), 0)),
            pl.BlockSpec((1, LANES), lambda b, c: (0, 0)),
            pl.BlockSpec((1, dv_all), lambda b, c: (0, 0)),
        ] + cast_in,
        out_specs=[pl.BlockSpec((L, dv_all), lambda b, c: (tok(b, c), 0))] + cast_out,
        out_shape=[jax.ShapeDtypeStruct((M, dv_all), _BF16)] + cast_shapes,
        scratch_shapes=[
            pltpu.VMEM((N_HEADS, dqk, dv), _F32),
            pltpu.VMEM((SUBLANES, dqk), _F32),
            pltpu.VMEM((SUBLANES, LANES), _F32),
        ],
        compiler_params=pltpu.CompilerParams(
            dimension_semantics=("arbitrary", "arbitrary"), vmem_limit_bytes=VMEM_LIMIT),
        name="mlstm",
    )(qkvo, qkvo, qkvo, qkvo, gates, gate_bias, gnorm, *[a for a, _, _ in cast_items])


def _out_proj_kernel(x_ref, yc_ref, ym_ref, wc_ref, wm_ref, o_ref):
    acc = _dot(yc_ref[...], wc_ref[...]) + _dot(ym_ref[...], wm_ref[...])
    o_ref[...] = x_ref[...] + acc


def _out_proj(x2d, yc, ym, w_out, *, tm=1024, tn=1024):
    M, D = x2d.shape
    kc = yc.shape[1]
    km = ym.shape[1]
    assert kc == km
    return pl.pallas_call(
        _out_proj_kernel,
        grid=(M // tm, D // tn),
        in_specs=[
            pl.BlockSpec((tm, tn), lambda m, j: (m, j)),
            pl.BlockSpec((tm, kc), lambda m, j: (m, 0)),
            pl.BlockSpec((tm, km), lambda m, j: (m, 0)),
            pl.BlockSpec((kc, tn), lambda m, j: (0, j)),
            pl.BlockSpec((km, tn), lambda m, j: (1, j)),
        ],
        out_specs=pl.BlockSpec((tm, tn), lambda m, j: (m, j)),
        out_shape=jax.ShapeDtypeStruct((M, D), _F32),
        compiler_params=pltpu.CompilerParams(
            dimension_semantics=("arbitrary", "arbitrary"), vmem_limit_bytes=VMEM_LIMIT),
        name="out_proj",
    )(x2d, yc, ym, w_out, w_out)


def _ffn_kernel(x_ref, g_ref, wg_ref, wu_ref, cwg_ref, cwu_ref, bg_ref, bu_ref, wd_ref,
                o_ref, h_ref, halog_ref, halou_ref, *, tiles_per_seq, rows):
    m = pl.program_id(0)
    j = pl.program_id(1)
    tm = x_ref.shape[0]

    @pl.when(j == 0)
    def _():
        x = x_ref[...]
        h_ref[...] = _rmsnorm_rows(x, g_ref[...]).astype(_BF16)
        o_ref[...] = x

    first = m % tiles_per_seq == 0
    prev_g = _load_halo(halog_ref, j, first)
    prev_u = _load_halo(halou_ref, j, first)

    def up_proj(r):
        h = h_ref[r * rows:(r + 1) * rows, :]
        return _dot(h, wg_ref[...]), _dot(h, wu_ref[...])

    n_chunks = tm // rows
    cur = up_proj(0)
    for r in range(n_chunks):
        nxt = up_proj(r + 1) if r + 1 < n_chunks else None
        zg, zu = cur
        gate = _causal_conv3(zg, prev_g, cwg_ref[...]) + bg_ref[...]
        up = _causal_conv3(zu, prev_u, cwu_ref[...]) + bu_ref[...]
        act = (jax.nn.silu(gate) * up).astype(_BF16)
        o_ref[r * rows:(r + 1) * rows, :] += _dot(act, wd_ref[...])
        prev_g = zg[rows - SUBLANES:, :]
        prev_u = zu[rows - SUBLANES:, :]
        cur = nxt
    halog_ref[j] = prev_g
    halou_ref[j] = prev_u


def _ffn(x2d, g, w_up, cw, cb, w_down, *, seq, tm=1024, tf=512, rows=256):
    M, D = x2d.shape
    F = w_down.shape[0]
    nj = F // tf
    return pl.pallas_call(
        functools.partial(_ffn_kernel, tiles_per_seq=seq // tm, rows=rows),
        grid=(M // tm, nj),
        in_specs=[
            pl.BlockSpec((tm, D), lambda m, j: (m, 0)),
            pl.BlockSpec((1, D), lambda m, j: (0, 0)),
            pl.BlockSpec((D, tf), lambda m, j: (0, j)),
            pl.BlockSpec((D, tf), lambda m, j: (0, nj + j)),
            pl.BlockSpec((CONV_WIDTH, tf), lambda m, j: (0, j)),
            pl.BlockSpec((CONV_WIDTH, tf), lambda m, j: (0, nj + j)),
            pl.BlockSpec((1, tf), lambda m, j: (0, j)),
            pl.BlockSpec((1, tf), lambda m, j: (0, nj + j)),
            pl.BlockSpec((tf, D), lambda m, j: (j, 0)),
        ],
        out_specs=pl.BlockSpec((tm, D), lambda m, j: (m, 0)),
        out_shape=jax.ShapeDtypeStruct((M, D), _F32),
        scratch_shapes=[
            pltpu.VMEM((tm, D), _BF16),
            pltpu.VMEM((nj, SUBLANES, tf), _F32),
            pltpu.VMEM((nj, SUBLANES, tf), _F32),
        ],
        compiler_params=pltpu.CompilerParams(
            dimension_semantics=("arbitrary", "arbitrary"), vmem_limit_bytes=VMEM_LIMIT),
        name="ffn",
    )(x2d, g, w_up, w_up, cw, cw, cb, cb, w_down)


def _ple_kernel(x_ref, g_ref, p_ref, wg_ref, wp_ref, gf_ref, o_ref, *, final_norm, rows):
    def normed(r):
        return _rmsnorm_rows(x_ref[r * rows:(r + 1) * rows, :], g_ref[...]).astype(_BF16)

    n_chunks = x_ref.shape[0] // rows
    h = normed(0)
    for r in range(n_chunks):
        h_next = normed(r + 1) if r + 1 < n_chunks else None
        sl = slice(r * rows, (r + 1) * rows)
        gate = jax.nn.sigmoid(_dot(h, wg_ref[...]))
        x = x_ref[sl, :] + gate * _dot(p_ref[sl, :].astype(_BF16), wp_ref[...])
        if final_norm:
            x = _rmsnorm_rows(x, gf_ref[...])
        o_ref[sl, :] = x
        h = h_next


def _ple(x2d, g, p2d, w_gate, w_proj, g_final, *, final_norm, tm=512, rows=256):
    M, D = x2d.shape
    P = p2d.shape[1]
    return pl.pallas_call(
        functools.partial(_ple_kernel, final_norm=final_norm, rows=rows),
        grid=(M // tm,),
        in_specs=[
            pl.BlockSpec((tm, D), lambda m: (m, 0)),
            pl.BlockSpec((1, D), lambda m: (0, 0)),
            pl.BlockSpec((tm, P), lambda m: (m, 0)),
            pl.BlockSpec((D, D), lambda m: (0, 0)),
            pl.BlockSpec((P, D), lambda m: (0, 0)),
            pl.BlockSpec((1, D), lambda m: (0, 0)),
        ],
        out_specs=pl.BlockSpec((tm, D), lambda m: (m, 0)),
        out_shape=jax.ShapeDtypeStruct((M, D), _F32),
        compiler_params=pltpu.CompilerParams(
            dimension_semantics=("arbitrary",), vmem_limit_bytes=VMEM_LIMIT),
        name="ple",
    )(x2d, g, p2d, w_gate, w_proj, g_final)


def kernel(x, p, norm_mix_g, w_in, b_igate, b_fgate, short_conv_w, mh_norm_g, w_out,
           norm_ffn_g, w_up, ffn_conv_w, ffn_conv_b, w_down, norm_ple_g, w_ple_gate,
           w_ple_proj, final_norm_g):
    B, S, D = x.shape
    depth = w_in.shape[0]
    d_conv = short_conv_w.shape[-1]
    d_mlstm = mh_norm_g.shape[-1]
    dv = d_mlstm // N_HEADS
    d_qk = (w_in.shape[-1] - 3 * d_conv - 2 * d_mlstm - 2 * N_HEADS) // 2
    dqk = d_qk // N_HEADS
    M = B * S

    x2d = x.reshape(M, D)
    row = lambda a: a.reshape(1, -1).astype(_F32)
    colscale = jnp.concatenate([
        jnp.full((1, d_qk), dqk ** -0.5, _F32),
        jnp.ones((1, d_qk + 2 * d_mlstm), _F32)], axis=1)

    for i in range(depth):
        w = w_in[i]
        c0 = 3 * d_conv
        c1 = c0 + 2 * d_qk + 2 * d_mlstm
        w_t = jnp.swapaxes(w, 0, 1)
        w_gates = jnp.pad(w_t[c1:], ((0, LANES - 2 * N_HEADS), (0, 0))).astype(_BF16)
        gate_bias = jnp.pad(jnp.concatenate([b_igate[i], b_fgate[i]]).astype(_F32),
                            (0, LANES - 2 * N_HEADS)).reshape(1, LANES)

        g_mix = row(norm_mix_g[i])
        y_conv, w_qkvo_t = _conv_group(x2d, g_mix, w_t, short_conv_w[i].astype(_F32),
                                       [(w_t, c0, c1 - c0)], seq=S)
        qkvo, gates = _qkvo(x2d, g_mix, w_qkvo_t, 0, colscale, w_gates)
        y_m, w_up_b, w_down_b, w_out_b, w_pg_b = _mlstm(
            qkvo, gates, gate_bias, row(mh_norm_g[i]),
            [(a, 0, a.shape[0]) for a in (w_up[i], w_down[i], w_out[i], w_ple_gate[i])],
            batch=B, seq=S, dqk=dqk, dv=dv)
        x2d = _out_proj(x2d, y_conv, y_m, w_out_b)
        x2d = _ffn(x2d, row(norm_ffn_g[i]), w_up_b, ffn_conv_w[i].astype(_F32),
                   row(ffn_conv_b[i]), w_down_b, seq=S)
        x2d = _ple(x2d, row(norm_ple_g[i]), p[i].reshape(M, -1), w_pg_b,
                   w_ple_proj[i].astype(_BF16), row(final_norm_g),
                   final_norm=(i == depth - 1))
    return x2d.reshape(B, S, D)
```

```python
import functools

import jax
import jax.numpy as jnp
from jax import lax
from jax.experimental import pallas as pl
from jax.experimental.pallas import tpu as pltpu

N_HEADS = 4
CONV_WIDTH = 3
EPS = 1e-6
LANES = 128
SUBLANES = 8
MLSTM_CHUNK = 256
W_COLS = 512
VMEM_LIMIT = 62 * 1024 * 1024

_BF16 = jnp.bfloat16
_F32 = jnp.float32


def _dot(a, b):
    return jnp.dot(a, b, preferred_element_type=_F32)


def _dot_nt(a, bt):
    return lax.dot_general(a, bt, (((1,), (1,)), ((), ())), preferred_element_type=_F32)


def _rmsnorm_rows(x, g):
    ms = jnp.mean(x * x, axis=-1, keepdims=True)
    return x * lax.rsqrt(ms + EPS) * g


def _causal_conv3(z, prev, cw):
    n = z.shape[0]
    ext = jnp.concatenate([prev, z], axis=0)
    z2 = ext[SUBLANES - 2:SUBLANES - 2 + n, :]
    z1 = ext[SUBLANES - 1:SUBLANES - 1 + n, :]
    return z2 * cw[0:1, :] + z1 * cw[1:2, :] + z * cw[2:3, :]


def _load_halo(halo_ref, j, first_tile):
    @pl.when(first_tile)
    def _():
        halo_ref[j] = jnp.zeros(halo_ref.shape[1:], _F32)

    return halo_ref[j]


def _cast_plan(items, n_steps, step_index):
    in_specs, out_specs, shapes = [], [], []
    for a, row0, n_rows in items:
        slab = n_rows // n_steps
        assert slab * n_steps == n_rows and slab % (2 * SUBLANES) == 0 and row0 % slab == 0
        cols = a.shape[1]
        in_specs.append(pl.BlockSpec(
            (slab, cols), lambda *idx, b0=row0 // slab: (b0 + step_index(*idx), 0)))
        out_specs.append(pl.BlockSpec((slab, cols), lambda *idx: (step_index(*idx), 0)))
        shapes.append(jax.ShapeDtypeStruct((n_rows, cols), _BF16))
    return in_specs, out_specs, shapes


def _cast_slabs(refs):
    half = len(refs) // 2
    for src, dst in zip(refs[:half], refs[half:]):
        dst[...] = src[...].astype(dst.dtype)


def _conv_group_kernel(x_ref, g_ref, wb_ref, wc_ref, wu_ref, cw_ref, *rest,
                       tiles_per_seq, rows, n_cast):
    cast_in, (o_ref, *cast_out) = rest[:n_cast], rest[n_cast:2 * n_cast + 1]
    h_ref, halo_ref = rest[2 * n_cast + 1:]
    m = pl.program_id(0)
    j = pl.program_id(1)
    tm = x_ref.shape[0]

    _cast_slabs(list(cast_in) + list(cast_out))

    @pl.when(j == 0)
    def _():
        h_ref[...] = _rmsnorm_rows(x_ref[...], g_ref[...]).astype(_BF16)

    wb = wb_ref[...].astype(_BF16)
    wc = wc_ref[...].astype(_BF16)
    wu = wu_ref[...].astype(_BF16)

    def proj(r):
        h = h_ref[r * rows:(r + 1) * rows, :]
        return _dot_nt(h, wb), _dot_nt(h, wc), _dot_nt(h, wu)

    prev = _load_halo(halo_ref, j, m % tiles_per_seq == 0)
    n_chunks = tm // rows
    cur = proj(0)
    for r in range(n_chunks):
        nxt = proj(r + 1) if r + 1 < n_chunks else None
        gb, gc, u = cur
        z = gc * u
        y = _causal_conv3(z, prev, cw_ref[...])
        o_ref[r * rows:(r + 1) * rows, :] = (gb * y).astype(o_ref.dtype)
        prev = z[rows - SUBLANES:, :]
        cur = nxt
    halo_ref[j] = prev


def _conv_group(x2d, g, w_in_t, cw, cast_items, *, seq, tm=1024, tc=256, rows=256):
    M, D = x2d.shape
    C = cw.shape[1]
    nj = C // tc
    n_steps = (M // tm) * nj
    cast_in, cast_out, cast_shapes = _cast_plan(cast_items, n_steps, lambda m, j: m * nj + j)
    return pl.pallas_call(
        functools.partial(_conv_group_kernel, tiles_per_seq=seq // tm, rows=rows,
                          n_cast=len(cast_items)),
        grid=(M // tm, nj),
        in_specs=[
            pl.BlockSpec((tm, D), lambda m, j: (m, 0)),
            pl.BlockSpec((1, D), lambda m, j: (0, 0)),
            pl.BlockSpec((tc, D), lambda m, j: (j, 0)),
            pl.BlockSpec((tc, D), lambda m, j: (nj + j, 0)),
            pl.BlockSpec((tc, D), lambda m, j: (2 * nj + j, 0)),
            pl.BlockSpec((CONV_WIDTH, tc), lambda m, j: (0, j)),
        ] + cast_in,
        out_specs=[pl.BlockSpec((tm, tc), lambda m, j: (m, j))] + cast_out,
        out_shape=[jax.ShapeDtypeStruct((M, C), _BF16)] + cast_shapes,
        scratch_shapes=[
            pltpu.VMEM((tm, D), _BF16),
            pltpu.VMEM((nj, SUBLANES, tc), _F32),
        ],
        compiler_params=pltpu.CompilerParams(
            dimension_semantics=("arbitrary", "arbitrary"), vmem_limit_bytes=VMEM_LIMIT),
        name="conv_group",
    )(x2d, g, w_in_t, w_in_t, w_in_t, cw, *[a for a, _, _ in cast_items])


def _qkvo_kernel(x_ref, g_ref, w_ref, cs_ref, wgate_ref, o_ref, gate_ref, h_ref):
    j = pl.program_id(1)

    @pl.when(j == 0)
    def _():
        h = _rmsnorm_rows(x_ref[...], g_ref[...]).astype(_BF16)
        h_ref[...] = h
        gate_ref[...] = _dot_nt(h, wgate_ref[...])

    o_ref[...] = (_dot_nt(h_ref[...], w_ref[...]) * cs_ref[...]).astype(o_ref.dtype)


def _qkvo(x2d, g, w_in_t, row0, colscale, wgate, *, tm=1024, tn=1024):
    M, D = x2d.shape
    N = colscale.shape[1]
    assert row0 % tn == 0 and N % tn == 0
    jb = row0 // tn
    return pl.pallas_call(
        _qkvo_kernel,
        grid=(M // tm, N // tn),
        in_specs=[
            pl.BlockSpec((tm, D), lambda m, j: (m, 0)),
            pl.BlockSpec((1, D), lambda m, j: (0, 0)),
            pl.BlockSpec((tn, D), lambda m, j: (jb + j, 0)),
            pl.BlockSpec((1, tn), lambda m, j: (0, j)),
            pl.BlockSpec((LANES, D), lambda m, j: (0, 0)),
        ],
        out_specs=[
            pl.BlockSpec((tm, tn), lambda m, j: (m, j)),
            pl.BlockSpec((tm, LANES), lambda m, j: (m, 0)),
        ],
        out_shape=[
            jax.ShapeDtypeStruct((M, N), _BF16),
            jax.ShapeDtypeStruct((M, LANES), _F32),
        ],
        scratch_shapes=[pltpu.VMEM((tm, D), _BF16)],
        compiler_params=pltpu.CompilerParams(
            dimension_semantics=("arbitrary", "arbitrary"), vmem_limit_bytes=VMEM_LIMIT),
        name="qkvo",
    )(x2d, g, w_in_t, colscale, wgate)


def _exact_cumsum_rows(tril, a):
    hi = a.astype(_BF16)
    r1 = a - hi.astype(_F32)
    mid = r1.astype(_BF16)
    lo = (r1 - mid.astype(_F32)).astype(_BF16)
    return _dot(tril, hi) + _dot(tril, mid) + _dot(tril, lo)


def _mlstm_kernel(q_ref, k_ref, v_ref, o_ref, gate_ref, bias_ref, gn_ref, *rest,
                  dqk, dv, n_cast):
    cast_in, (y_ref, *cast_out) = rest[:n_cast], rest[n_cast:2 * n_cast + 1]
    c_ref, n_ref, m_ref = rest[2 * n_cast + 1:]
    bb, L = q_ref.shape[0], q_ref.shape[1]

    _cast_slabs(list(cast_in) + list(cast_out))

    @pl.when(pl.program_id(1) == 0)
    def _():
        c_ref[...] = jnp.zeros(c_ref.shape, _F32)
        n_ref[...] = jnp.zeros(n_ref.shape, _F32)
        m_ref[...] = jnp.zeros(m_ref.shape, _F32)

    row = lax.broadcasted_iota(jnp.int32, (L, L), 0)
    col = lax.broadcasted_iota(jnp.int32, (L, L), 1)
    causal = col <= row
    tril = causal.astype(_BF16)

    chains = [(bi, h) for bi in range(bb) for h in range(N_HEADS)]
    pre = {}
    for bi in range(bb):
        gates = gate_ref[bi] + bias_ref[...]
        logf = jnp.minimum(gates, 0.0) - jnp.log1p(jnp.exp(-jnp.abs(gates)))
        bcum = _exact_cumsum_rows(tril, logf)
        pre[bi] = (gates, bcum, gates.T, bcum.T)

    stab = {}
    for bi, h in chains:
        gates, bcum, gates_t, bcum_t = pre[bi]
        st = bi * N_HEADS + h
        b_col = bcum[:, N_HEADS + h:N_HEADS + h + 1]
        r_row = gates_t[h:h + 1, :] - bcum_t[N_HEADS + h:N_HEADS + h + 1, :]
        dlog = jnp.where(causal, b_col + r_row, -jnp.inf)
        inter = b_col + m_ref[st:st + 1, 0:1]
        m_t = jnp.maximum(inter, jnp.max(dlog, axis=-1, keepdims=True))
        stab[bi, h] = (b_col, dlog, inter, m_t)

    out = {}
    for bi, h in chains:
        st = bi * N_HEADS + h
        b_col, dlog, inter, m_t = stab[bi, h]
        dw = jnp.exp(dlog - m_t)
        a_inter = jnp.exp(inter - m_t)
        q = q_ref[bi, :, h * dqk:(h + 1) * dqk]
        k = k_ref[bi, :, h * dqk:(h + 1) * dqk]
        v = v_ref[bi, :, h * dv:(h + 1) * dv]
        s = _dot_nt(q, k) * dw
        num = a_inter * _dot(q, c_ref[st].astype(_BF16)) + _dot(s.astype(_BF16), v)
        qn = jnp.sum(q.astype(_F32) * n_ref[st:st + 1, :], axis=-1, keepdims=True)
        den = a_inter * qn + jnp.sum(s, axis=-1, keepdims=True)
        den = jnp.maximum(jnp.abs(den), jnp.exp(-m_t))
        out[bi, h] = (num / den, a_inter)

    for bi, h in chains:
        hm, _ = out[bi, h]
        hm = hm * lax.rsqrt(jnp.mean(hm * hm, axis=-1, keepdims=True) + EPS)
        hm = hm * gn_ref[:, h * dv:(h + 1) * dv]
        og = jax.nn.sigmoid(o_ref[bi, :, h * dv:(h + 1) * dv].astype(_F32))
        y_ref[bi, :, h * dv:(h + 1) * dv] = (og * hm).astype(y_ref.dtype)

    for bi, h in chains:
        st = bi * N_HEADS + h
        gates = pre[bi][0]
        b_col, _, _, m_t = stab[bi, h]
        a_inter = out[bi, h][1]
        k = k_ref[bi, :, h * dqk:(h + 1) * dqk]
        v = v_ref[bi, :, h * dv:(h + 1) * dv]
        b_tot = b_col[L - 1:L, :]
        m_new = m_t[L - 1:L, :]
        a_prev = a_inter[L - 1:L, :]
        w_col = jnp.exp(b_tot - b_col + gates[:, h:h + 1] - m_new)
        kw = k.astype(_F32) * w_col
        c_ref[st] = a_prev * c_ref[st] + _dot(kw.T.astype(_BF16), v)
        n_ref[st:st + 1, :] = a_prev * n_ref[st:st + 1, :] + jnp.sum(kw, axis=0, keepdims=True)
        m_ref[st:st + 1, :] = jnp.broadcast_to(m_new, (1, LANES))


def _mlstm(qkvo, gates, gate_bias, gnorm, cast_items, *, batch, seq, dqk, dv, bb=2):
    L = MLSTM_CHUNK
    nc = seq // L
    dq_all = N_HEADS * dqk
    dv_all = N_HEADS * dv
    assert dv_all == 2 * dq_all and batch % bb == 0 and bb * N_HEADS <= SUBLANES
    qkvo3 = qkvo.reshape(batch, seq, -1)
    gates3 = gates.reshape(batch, seq, LANES)
    cast_in, cast_out, cast_shapes = _cast_plan(
        cast_items, (batch // bb) * nc, lambda b, c: b * nc + c)
    y, *cast = pl.pallas_call(
        functools.partial(_mlstm_kernel, dqk=dqk, dv=dv, n_cast=len(cast_items)),
        grid=(batch // bb, nc),
        in_specs=[
            pl.BlockSpec((bb, L, dq_all), lambda b, c: (b, c, 0)),
            pl.BlockSpec((bb, L, dq_all), lambda b, c: (b, c, 1)),
            pl.BlockSpec((bb, L, dv_all), lambda b, c: (b, c, 1)),
            pl.BlockSpec((bb, L, dv_all), lambda b, c: (b, c, 2)),
            pl.BlockSpec((bb, L, LANES), lambda b, c: (b, c, 0)),
            pl.BlockSpec((1, LANES), lambda b, c: (0, 0)),
            pl.BlockSpec((1, dv_all), lambda b, c: (0, 0)),
        ] + cast_in,
        out_specs=[pl.BlockSpec((bb, L, dv_all), lambda b, c: (b, c, 0))] + cast_out,
        out_shape=[jax.ShapeDtypeStruct((batch, seq, dv_all), _BF16)] + cast_shapes,
        scratch_shapes=[
            pltpu.VMEM((bb * N_HEADS, dqk, dv), _F32),
            pltpu.VMEM((SUBLANES, dqk), _F32),
            pltpu.VMEM((SUBLANES, LANES), _F32),
        ],
        compiler_params=pltpu.CompilerParams(
            dimension_semantics=("arbitrary", "arbitrary"), vmem_limit_bytes=VMEM_LIMIT),
        name="mlstm",
    )(qkvo3, qkvo3, qkvo3, qkvo3, gates3, gate_bias, gnorm, *[a for a, _, _ in cast_items])
    return (y.reshape(batch * seq, dv_all), *cast)


def _out_proj_kernel(x_ref, yc_ref, ym_ref, wc_ref, wm_ref, o_ref):
    acc = _dot(yc_ref[...], wc_ref[...]) + _dot(ym_ref[...], wm_ref[...])
    o_ref[...] = x_ref[...] + acc


def _out_proj(x2d, yc, ym, w_out, *, tm=1024, tn=1024):
    M, D = x2d.shape
    kc = yc.shape[1]
    km = ym.shape[1]
    assert kc == km
    return pl.pallas_call(
        _out_proj_kernel,
        grid=(M // tm, D // tn),
        in_specs=[
            pl.BlockSpec((tm, tn), lambda m, j: (m, j)),
            pl.BlockSpec((tm, kc), lambda m, j: (m, 0)),
            pl.BlockSpec((tm, km), lambda m, j: (m, 0)),
            pl.BlockSpec((kc, tn), lambda m, j: (0, j)),
            pl.BlockSpec((km, tn), lambda m, j: (1, j)),
        ],
        out_specs=pl.BlockSpec((tm, tn), lambda m, j: (m, j)),
        out_shape=jax.ShapeDtypeStruct((M, D), _F32),
        compiler_params=pltpu.CompilerParams(
            dimension_semantics=("arbitrary", "arbitrary"), vmem_limit_bytes=VMEM_LIMIT),
        name="out_proj",
    )(x2d, yc, ym, w_out, w_out)


def _ffn_kernel(x_ref, g_ref, wg_ref, wu_ref, cwg_ref, cwu_ref, bg_ref, bu_ref, wd_ref,
                o_ref, h_ref, halog_ref, halou_ref, *, tiles_per_seq, rows):
    m = pl.program_id(0)
    j = pl.program_id(1)
    tm = x_ref.shape[0]

    @pl.when(j == 0)
    def _():
        x = x_ref[...]
        h_ref[...] = _rmsnorm_rows(x, g_ref[...]).astype(_BF16)
        o_ref[...] = x

    first = m % tiles_per_seq == 0
    prev_g = _load_halo(halog_ref, j, first)
    prev_u = _load_halo(halou_ref, j, first)

    def up_proj(r):
        h = h_ref[r * rows:(r + 1) * rows, :]
        return _dot(h, wg_ref[...]), _dot(h, wu_ref[...])

    n_chunks = tm // rows
    cur = up_proj(0)
    for r in range(n_chunks):
        nxt = up_proj(r + 1) if r + 1 < n_chunks else None
        zg, zu = cur
        gate = _causal_conv3(zg, prev_g, cwg_ref[...]) + bg_ref[...]
        up = _causal_conv3(zu, prev_u, cwu_ref[...]) + bu_ref[...]
        act = (jax.nn.silu(gate) * up).astype(_BF16)
        o_ref[r * rows:(r + 1) * rows, :] += _dot(act, wd_ref[...])
        prev_g = zg[rows - SUBLANES:, :]
        prev_u = zu[rows - SUBLANES:, :]
        cur = nxt
    halog_ref[j] = prev_g
    halou_ref[j] = prev_u


def _ffn(x2d, g, w_up, cw, cb, w_down, *, seq, tm=1024, tf=512, rows=512):
    M, D = x2d.shape
    F = w_down.shape[0]
    nj = F // tf
    return pl.pallas_call(
        functools.partial(_ffn_kernel, tiles_per_seq=seq // tm, rows=rows),
        grid=(M // tm, nj),
        in_specs=[
            pl.BlockSpec((tm, D), lambda m, j: (m, 0)),
            pl.BlockSpec((1, D), lambda m, j: (0, 0)),
            pl.BlockSpec((D, tf), lambda m, j: (0, j)),
            pl.BlockSpec((D, tf), lambda m, j: (0, nj + j)),
            pl.BlockSpec((CONV_WIDTH, tf), lambda m, j: (0, j)),
            pl.BlockSpec((CONV_WIDTH, tf), lambda m, j: (0, nj + j)),
            pl.BlockSpec((1, tf), lambda m, j: (0, j)),
            pl.BlockSpec((1, tf), lambda m, j: (0, nj + j)),
            pl.BlockSpec((tf, D), lambda m, j: (j, 0)),
        ],
        out_specs=pl.BlockSpec((tm, D), lambda m, j: (m, 0)),
        out_shape=jax.ShapeDtypeStruct((M, D), _F32),
        scratch_shapes=[
            pltpu.VMEM((tm, D), _BF16),
            pltpu.VMEM((nj, SUBLANES, tf), _F32),
            pltpu.VMEM((nj, SUBLANES, tf), _F32),
        ],
        compiler_params=pltpu.CompilerParams(
            dimension_semantics=("arbitrary", "arbitrary"), vmem_limit_bytes=VMEM_LIMIT),
        name="ffn",
    )(x2d, g, w_up, w_up, cw, cw, cb, cb, w_down)


def _ple_kernel(x_ref, g_ref, p_ref, wg_ref, wp_ref, gf_ref, o_ref, *, final_norm, rows):
    def normed(r):
        return _rmsnorm_rows(x_ref[r * rows:(r + 1) * rows, :], g_ref[...]).astype(_BF16)

    n_chunks = x_ref.shape[0] // rows
    h = normed(0)
    for r in range(n_chunks):
        h_next = normed(r + 1) if r + 1 < n_chunks else None
        sl = slice(r * rows, (r + 1) * rows)
        gate = jax.nn.sigmoid(_dot(h, wg_ref[...]))
        x = x_ref[sl, :] + gate * _dot(p_ref[sl, :].astype(_BF16), wp_ref[...])
        if final_norm:
            x = _rmsnorm_rows(x, gf_ref[...])
        o_ref[sl, :] = x
        h = h_next


def _ple(x2d, g, p2d, w_gate, w_proj, g_final, *, final_norm, tm=512, rows=256):
    M, D = x2d.shape
    P = p2d.shape[1]
    return pl.pallas_call(
        functools.partial(_ple_kernel, final_norm=final_norm, rows=rows),
        grid=(M // tm,),
        in_specs=[
            pl.BlockSpec((tm, D), lambda m: (m, 0)),
            pl.BlockSpec((1, D), lambda m: (0, 0)),
            pl.BlockSpec((tm, P), lambda m: (m, 0)),
            pl.BlockSpec((D, D), lambda m: (0, 0)),
            pl.BlockSpec((P, D), lambda m: (0, 0)),
            pl.BlockSpec((1, D), lambda m: (0, 0)),
        ],
        out_specs=pl.BlockSpec((tm, D), lambda m: (m, 0)),
        out_shape=jax.ShapeDtypeStruct((M, D), _F32),
        compiler_params=pltpu.CompilerParams(
            dimension_semantics=("arbitrary",), vmem_limit_bytes=VMEM_LIMIT),
        name="ple",
    )(x2d, g, p2d, w_gate, w_proj, g_final)


def kernel(x, p, norm_mix_g, w_in, b_igate, b_fgate, short_conv_w, mh_norm_g, w_out,
           norm_ffn_g, w_up, ffn_conv_w, ffn_conv_b, w_down, norm_ple_g, w_ple_gate,
           w_ple_proj, final_norm_g):
    B, S, D = x.shape
    depth = w_in.shape[0]
    d_conv = short_conv_w.shape[-1]
    d_mlstm = mh_norm_g.shape[-1]
    dv = d_mlstm // N_HEADS
    d_qk = (w_in.shape[-1] - 3 * d_conv - 2 * d_mlstm - 2 * N_HEADS) // 2
    dqk = d_qk // N_HEADS
    M = B * S

    x2d = x.reshape(M, D)
    row = lambda a: a.reshape(1, -1).astype(_F32)
    colscale = jnp.concatenate([
        jnp.full((1, d_qk), dqk ** -0.5, _F32),
        jnp.ones((1, d_qk + 2 * d_mlstm), _F32)], axis=1)

    for i in range(depth):
        w = w_in[i]
        c0 = 3 * d_conv
        c1 = c0 + 2 * d_qk + 2 * d_mlstm
        w_t = jnp.swapaxes(w, 0, 1)
        w_gates = jnp.pad(w_t[c1:], ((0, LANES - 2 * N_HEADS), (0, 0))).astype(_BF16)
        gate_bias = jnp.pad(jnp.concatenate([b_igate[i], b_fgate[i]]).astype(_F32),
                            (0, LANES - 2 * N_HEADS)).reshape(1, LANES)

        g_mix = row(norm_mix_g[i])
        y_conv, w_qkvo_t = _conv_group(x2d, g_mix, w_t, short_conv_w[i].astype(_F32),
                                       [(w_t, c0, c1 - c0)], seq=S)
        qkvo, gates = _qkvo(x2d, g_mix, w_qkvo_t, 0, colscale, w_gates)
        y_m, w_up_b, w_down_b, w_out_b, w_pg_b = _mlstm(
            qkvo, gates, gate_bias, row(mh_norm_g[i]),
            [(a, 0, a.shape[0]) for a in (w_up[i], w_down[i], w_out[i], w_ple_gate[i])],
            batch=B, seq=S, dqk=dqk, dv=dv)
        x2d = _out_proj(x2d, y_conv, y_m, w_out_b)
        x2d = _ffn(x2d, row(norm_ffn_g[i]), w_up_b, ffn_conv_w[i].astype(_F32),
                   row(ffn_conv_b[i]), w_down_b, seq=S)
        x2d = _ple(x2d, row(norm_ple_g[i]), p[i].reshape(M, -1), w_pg_b,
                   w_ple_proj[i].astype(_BF16), row(final_norm_g),
                   final_norm=(i == depth - 1))
    return x2d.reshape(B, S, D)
```

```python
import functools

import jax
import jax.numpy as jnp
from jax import lax
from jax.experimental import pallas as pl
from jax.experimental.pallas import tpu as pltpu

N_HEADS = 4
CONV_WIDTH = 3
EPS = 1e-6
LANES = 128
SUBLANES = 8
MLSTM_CHUNK = 256
W_COLS = 512
VMEM_LIMIT = 62 * 1024 * 1024

_BF16 = jnp.bfloat16
_F32 = jnp.float32


def _dot(a, b):
    return jnp.dot(a, b, preferred_element_type=_F32)


def _dot_nt(a, bt):
    return lax.dot_general(a, bt, (((1,), (1,)), ((), ())), preferred_element_type=_F32)


def _rmsnorm_rows(x, g):
    ms = jnp.mean(x * x, axis=-1, keepdims=True)
    return x * lax.rsqrt(ms + EPS) * g


def _causal_conv3(z, prev, cw):
    n = z.shape[0]
    ext = jnp.concatenate([prev, z], axis=0)
    z2 = ext[SUBLANES - 2:SUBLANES - 2 + n, :]
    z1 = ext[SUBLANES - 1:SUBLANES - 1 + n, :]
    return z2 * cw[0:1, :] + z1 * cw[1:2, :] + z * cw[2:3, :]


def _load_halo(halo_ref, j, first_tile):
    @pl.when(first_tile)
    def _():
        halo_ref[j] = jnp.zeros(halo_ref.shape[1:], _F32)

    return halo_ref[j]


def _cast_plan(items, n_steps, step_index):
    in_specs, out_specs, shapes = [], [], []
    for a, row0, n_rows in items:
        slab = n_rows // n_steps
        assert slab * n_steps == n_rows and slab % (2 * SUBLANES) == 0 and row0 % slab == 0
        cols = a.shape[1]
        in_specs.append(pl.BlockSpec(
            (slab, cols), lambda *idx, b0=row0 // slab: (b0 + step_index(*idx), 0)))
        out_specs.append(pl.BlockSpec((slab, cols), lambda *idx: (step_index(*idx), 0)))
        shapes.append(jax.ShapeDtypeStruct((n_rows, cols), _BF16))
    return in_specs, out_specs, shapes


def _cast_slabs(refs):
    half = len(refs) // 2
    for src, dst in zip(refs[:half], refs[half:]):
        dst[...] = src[...].astype(dst.dtype)


def _conv_group_kernel(x_ref, g_ref, wb_ref, wc_ref, wu_ref, cw_ref, *rest,
                       tiles_per_seq, rows, n_cast):
    cast_in, (o_ref, *cast_out) = rest[:n_cast], rest[n_cast:2 * n_cast + 1]
    h_ref, halo_ref, w_ref = rest[2 * n_cast + 1:]
    m = pl.program_id(0)
    j = pl.program_id(1)
    tm = x_ref.shape[0]

    _cast_slabs(list(cast_in) + list(cast_out))

    @pl.when(j == 0)
    def _():
        h_ref[...] = _rmsnorm_rows(x_ref[...], g_ref[...]).astype(_BF16)

    @pl.when(m == 0)
    def _():
        w_ref[3 * j] = wb_ref[...].astype(_BF16)
        w_ref[3 * j + 1] = wc_ref[...].astype(_BF16)
        w_ref[3 * j + 2] = wu_ref[...].astype(_BF16)

    def proj(r):
        h = h_ref[r * rows:(r + 1) * rows, :]
        return (_dot_nt(h, w_ref[3 * j]), _dot_nt(h, w_ref[3 * j + 1]),
                _dot_nt(h, w_ref[3 * j + 2]))

    prev = _load_halo(halo_ref, j, m % tiles_per_seq == 0)
    n_chunks = tm // rows
    cur = proj(0)
    for r in range(n_chunks):
        nxt = proj(r + 1) if r + 1 < n_chunks else None
        gb, gc, u = cur
        z = gc * u
        y = _causal_conv3(z, prev, cw_ref[...])
        o_ref[r * rows:(r + 1) * rows, :] = (gb * y).astype(o_ref.dtype)
        prev = z[rows - SUBLANES:, :]
        cur = nxt
    halo_ref[j] = prev


def _conv_group(x2d, g, w_in_t, cw, cast_items, *, seq, tm=1024, tc=256, rows=256):
    M, D = x2d.shape
    C = cw.shape[1]
    nj = C // tc
    n_steps = (M // tm) * nj
    cast_in, cast_out, cast_shapes = _cast_plan(cast_items, n_steps, lambda m, j: m * nj + j)
    wtile = lambda m, j: jnp.where(m == 0, j, nj - 1)
    return pl.pallas_call(
        functools.partial(_conv_group_kernel, tiles_per_seq=seq // tm, rows=rows,
                          n_cast=len(cast_items)),
        grid=(M // tm, nj),
        in_specs=[
            pl.BlockSpec((tm, D), lambda m, j: (m, 0)),
            pl.BlockSpec((1, D), lambda m, j: (0, 0)),
            pl.BlockSpec((tc, D), lambda m, j: (wtile(m, j), 0)),
            pl.BlockSpec((tc, D), lambda m, j: (nj + wtile(m, j), 0)),
            pl.BlockSpec((tc, D), lambda m, j: (2 * nj + wtile(m, j), 0)),
            pl.BlockSpec((CONV_WIDTH, tc), lambda m, j: (0, j)),
        ] + cast_in,
        out_specs=[pl.BlockSpec((tm, tc), lambda m, j: (m, j))] + cast_out,
        out_shape=[jax.ShapeDtypeStruct((M, C), _BF16)] + cast_shapes,
        scratch_shapes=[
            pltpu.VMEM((tm, D), _BF16),
            pltpu.VMEM((nj, SUBLANES, tc), _F32),
            pltpu.VMEM((3 * nj, tc, D), _BF16),
        ],
        compiler_params=pltpu.CompilerParams(
            dimension_semantics=("arbitrary", "arbitrary"), vmem_limit_bytes=VMEM_LIMIT),
        name="conv_group",
    )(x2d, g, w_in_t, w_in_t, w_in_t, cw, *[a for a, _, _ in cast_items])


def _qkvo_kernel(x_ref, g_ref, w_ref, cs_ref, wgate_ref, o_ref, gate_ref, h_ref):
    j = pl.program_id(1)

    @pl.when(j == 0)
    def _():
        h = _rmsnorm_rows(x_ref[...], g_ref[...]).astype(_BF16)
        h_ref[...] = h
        gate_ref[...] = _dot_nt(h, wgate_ref[...])

    tn = o_ref.shape[1]
    w = w_ref[pl.ds(pl.multiple_of(j * tn, tn), tn), :]
    o_ref[...] = (_dot_nt(h_ref[...], w) * cs_ref[...]).astype(o_ref.dtype)


def _qkvo(x2d, g, w_t, colscale, wgate, *, tm=1024, tn=1024):
    M, D = x2d.shape
    N = colscale.shape[1]
    assert w_t.shape == (N, D) and N % tn == 0
    return pl.pallas_call(
        _qkvo_kernel,
        grid=(M // tm, N // tn),
        in_specs=[
            pl.BlockSpec((tm, D), lambda m, j: (m, 0)),
            pl.BlockSpec((1, D), lambda m, j: (0, 0)),
            pl.BlockSpec((N, D), lambda m, j: (0, 0), pipeline_mode=pl.Buffered(1)),
            pl.BlockSpec((1, tn), lambda m, j: (0, j)),
            pl.BlockSpec((LANES, D), lambda m, j: (0, 0)),
        ],
        out_specs=[
            pl.BlockSpec((tm, tn), lambda m, j: (m, j)),
            pl.BlockSpec((tm, LANES), lambda m, j: (m, 0)),
        ],
        out_shape=[
            jax.ShapeDtypeStruct((M, N), _BF16),
            jax.ShapeDtypeStruct((M, LANES), _F32),
        ],
        scratch_shapes=[pltpu.VMEM((tm, D), _BF16)],
        compiler_params=pltpu.CompilerParams(
            dimension_semantics=("arbitrary", "arbitrary"), vmem_limit_bytes=VMEM_LIMIT),
        name="qkvo",
    )(x2d, g, w_t, colscale, wgate)


def _exact_cumsum_rows(tril, a):
    hi = a.astype(_BF16)
    r1 = a - hi.astype(_F32)
    mid = r1.astype(_BF16)
    lo = (r1 - mid.astype(_F32)).astype(_BF16)
    return _dot(tril, hi) + _dot(tril, mid) + _dot(tril, lo)


def _mlstm_kernel(q_ref, k_ref, v_ref, o_ref, gate_ref, bias_ref, gn_ref, *rest,
                  dqk, dv, n_cast):
    cast_in, (y_ref, *cast_out) = rest[:n_cast], rest[n_cast:2 * n_cast + 1]
    c_ref, n_ref, m_ref = rest[2 * n_cast + 1:]
    bb, L = q_ref.shape[0], q_ref.shape[1]

    _cast_slabs(list(cast_in) + list(cast_out))

    @pl.when(pl.program_id(1) == 0)
    def _():
        c_ref[...] = jnp.zeros(c_ref.shape, _F32)
        n_ref[...] = jnp.zeros(n_ref.shape, _F32)
        m_ref[...] = jnp.zeros(m_ref.shape, _F32)

    row = lax.broadcasted_iota(jnp.int32, (L, L), 0)
    col = lax.broadcasted_iota(jnp.int32, (L, L), 1)
    causal = col <= row
    tril = causal.astype(_BF16)

    chains = [(bi, h) for bi in range(bb) for h in range(N_HEADS)]
    pre = {}
    for bi in range(bb):
        gates = gate_ref[bi] + bias_ref[...]
        logf = jnp.minimum(gates, 0.0) - jnp.log1p(jnp.exp(-jnp.abs(gates)))
        bcum = _exact_cumsum_rows(tril, logf)
        pre[bi] = (gates, bcum, gates.T, bcum.T)

    stab = {}
    for bi, h in chains:
        gates, bcum, gates_t, bcum_t = pre[bi]
        st = bi * N_HEADS + h
        b_col = bcum[:, N_HEADS + h:N_HEADS + h + 1]
        r_row = gates_t[h:h + 1, :] - bcum_t[N_HEADS + h:N_HEADS + h + 1, :]
        dlog = jnp.where(causal, b_col + r_row, -jnp.inf)
        inter = b_col + m_ref[st:st + 1, 0:1]
        m_t = jnp.maximum(inter, jnp.max(dlog, axis=-1, keepdims=True))
        stab[bi, h] = (b_col, dlog, inter, m_t)

    out = {}
    for bi, h in chains:
        st = bi * N_HEADS + h
        b_col, dlog, inter, m_t = stab[bi, h]
        dw = jnp.exp(dlog - m_t)
        a_inter = jnp.exp(inter - m_t)
        q = q_ref[bi, :, h * dqk:(h + 1) * dqk]
        k = k_ref[bi, :, h * dqk:(h + 1) * dqk]
        v = v_ref[bi, :, h * dv:(h + 1) * dv]
        s = _dot_nt(q, k) * dw
        num = a_inter * _dot(q, c_ref[st].astype(_BF16)) + _dot(s.astype(_BF16), v)
        qn = jnp.sum(q.astype(_F32) * n_ref[st:st + 1, :], axis=-1, keepdims=True)
        den = a_inter * qn + jnp.sum(s, axis=-1, keepdims=True)
        den = jnp.maximum(jnp.abs(den), jnp.exp(-m_t))
        out[bi, h] = (num / den, a_inter)

    for bi, h in chains:
        hm, _ = out[bi, h]
        hm = hm * lax.rsqrt(jnp.mean(hm * hm, axis=-1, keepdims=True) + EPS)
        hm = hm * gn_ref[:, h * dv:(h + 1) * dv]
        og = jax.nn.sigmoid(o_ref[bi, :, h * dv:(h + 1) * dv].astype(_F32))
        y_ref[bi, :, h * dv:(h + 1) * dv] = (og * hm).astype(y_ref.dtype)

    for bi, h in chains:
        st = bi * N_HEADS + h
        gates = pre[bi][0]
        b_col, _, _, m_t = stab[bi, h]
        a_inter = out[bi, h][1]
        k = k_ref[bi, :, h * dqk:(h + 1) * dqk]
        v = v_ref[bi, :, h * dv:(h + 1) * dv]
        b_tot = b_col[L - 1:L, :]
        m_new = m_t[L - 1:L, :]
        a_prev = a_inter[L - 1:L, :]
        w_col = jnp.exp(b_tot - b_col + gates[:, h:h + 1] - m_new)
        kw = k.astype(_F32) * w_col
        c_ref[st] = a_prev * c_ref[st] + _dot(kw.T.astype(_BF16), v)
        n_ref[st:st + 1, :] = a_prev * n_ref[st:st + 1, :] + jnp.sum(kw, axis=0, keepdims=True)
        m_ref[st:st + 1, :] = jnp.broadcast_to(m_new, (1, LANES))


def _mlstm(qkvo, gates, gate_bias, gnorm, cast_items, *, batch, seq, dqk, dv, bb=2):
    L = MLSTM_CHUNK
    nc = seq // L
    dq_all = N_HEADS * dqk
    dv_all = N_HEADS * dv
    assert dv_all == 2 * dq_all and batch % bb == 0 and bb * N_HEADS <= SUBLANES
    qkvo3 = qkvo.reshape(batch, seq, -1)
    gates3 = gates.reshape(batch, seq, LANES)
    cast_in, cast_out, cast_shapes = _cast_plan(
        cast_items, (batch // bb) * nc, lambda b, c: b * nc + c)
    y, *cast = pl.pallas_call(
        functools.partial(_mlstm_kernel, dqk=dqk, dv=dv, n_cast=len(cast_items)),
        grid=(batch // bb, nc),
        in_specs=[
            pl.BlockSpec((bb, L, dq_all), lambda b, c: (b, c, 0)),
            pl.BlockSpec((bb, L, dq_all), lambda b, c: (b, c, 1)),
            pl.BlockSpec((bb, L, dv_all), lambda b, c: (b, c, 1)),
            pl.BlockSpec((bb, L, dv_all), lambda b, c: (b, c, 2)),
            pl.BlockSpec((bb, L, LANES), lambda b, c: (b, c, 0)),
            pl.BlockSpec((1, LANES), lambda b, c: (0, 0)),
            pl.BlockSpec((1, dv_all), lambda b, c: (0, 0)),
        ] + cast_in,
        out_specs=[pl.BlockSpec((bb, L, dv_all), lambda b, c: (b, c, 0))] + cast_out,
        out_shape=[jax.ShapeDtypeStruct((batch, seq, dv_all), _BF16)] + cast_shapes,
        scratch_shapes=[
            pltpu.VMEM((bb * N_HEADS, dqk, dv), _F32),
            pltpu.VMEM((SUBLANES, dqk), _F32),
            pltpu.VMEM((SUBLANES, LANES), _F32),
        ],
        compiler_params=pltpu.CompilerParams(
            dimension_semantics=("arbitrary", "arbitrary"), vmem_limit_bytes=VMEM_LIMIT),
        name="mlstm",
    )(qkvo3, qkvo3, qkvo3, qkvo3, gates3, gate_bias, gnorm, *[a for a, _, _ in cast_items])
    return (y.reshape(batch * seq, dv_all), *cast)


def _out_proj_kernel(x_ref, yc_ref, ym_ref, wc_ref, wm_ref, o_ref):
    acc = _dot(yc_ref[...], wc_ref[...]) + _dot(ym_ref[...], wm_ref[...])
    o_ref[...] = x_ref[...] + acc


def _out_proj(x2d, yc, ym, w_out, *, tm=1024, tn=1024):
    M, D = x2d.shape
    kc = yc.shape[1]
    km = ym.shape[1]
    assert kc == km
    return pl.pallas_call(
        _out_proj_kernel,
        grid=(M // tm, D // tn),
        in_specs=[
            pl.BlockSpec((tm, tn), lambda m, j: (m, j)),
            pl.BlockSpec((tm, kc), lambda m, j: (m, 0)),
            pl.BlockSpec((tm, km), lambda m, j: (m, 0)),
            pl.BlockSpec((kc, tn), lambda m, j: (0, j)),
            pl.BlockSpec((km, tn), lambda m, j: (1, j)),
        ],
        out_specs=pl.BlockSpec((tm, tn), lambda m, j: (m, j)),
        out_shape=jax.ShapeDtypeStruct((M, D), _F32),
        compiler_params=pltpu.CompilerParams(
            dimension_semantics=("arbitrary", "arbitrary"), vmem_limit_bytes=VMEM_LIMIT),
        name="out_proj",
    )(x2d, yc, ym, w_out, w_out)


def _ffn_kernel(x_ref, g_ref, wg_ref, wu_ref, cwg_ref, cwu_ref, bg_ref, bu_ref, wd_ref,
                o_ref, h_ref, halog_ref, halou_ref, *, tiles_per_seq, rows):
    m = pl.program_id(0)
    j = pl.program_id(1)
    tm = x_ref.shape[0]

    @pl.when(j == 0)
    def _():
        x = x_ref[...]
        h_ref[...] = _rmsnorm_rows(x, g_ref[...]).astype(_BF16)
        o_ref[...] = x

    first = m % tiles_per_seq == 0
    prev_g = _load_halo(halog_ref, j, first)
    prev_u = _load_halo(halou_ref, j, first)

    def up_proj(r):
        h = h_ref[r * rows:(r + 1) * rows, :]
        return _dot(h, wg_ref[...]), _dot(h, wu_ref[...])

    n_chunks = tm // rows
    cur = up_proj(0)
    for r in range(n_chunks):
        nxt = up_proj(r + 1) if r + 1 < n_chunks else None
        zg, zu = cur
        gate = _causal_conv3(zg, prev_g, cwg_ref[...]) + bg_ref[...]
        up = _causal_conv3(zu, prev_u, cwu_ref[...]) + bu_ref[...]
        act = (jax.nn.silu(gate) * up).astype(_BF16)
        o_ref[r * rows:(r + 1) * rows, :] += _dot(act, wd_ref[...])
        prev_g = zg[rows - SUBLANES:, :]
        prev_u = zu[rows - SUBLANES:, :]
        cur = nxt
    halog_ref[j] = prev_g
    halou_ref[j] = prev_u


def _ffn(x2d, g, w_up, cw, cb, w_down, *, seq, tm=1024, tf=512, rows=512):
    M, D = x2d.shape
    F = w_down.shape[0]
    nj = F // tf
    return pl.pallas_call(
        functools.partial(_ffn_kernel, tiles_per_seq=seq // tm, rows=rows),
        grid=(M // tm, nj),
        in_specs=[
            pl.BlockSpec((tm, D), lambda m, j: (m, 0)),
            pl.BlockSpec((1, D), lambda m, j: (0, 0)),
            pl.BlockSpec((D, tf), lambda m, j: (0, j)),
            pl.BlockSpec((D, tf), lambda m, j: (0, nj + j)),
            pl.BlockSpec((CONV_WIDTH, tf), lambda m, j: (0, j)),
            pl.BlockSpec((CONV_WIDTH, tf), lambda m, j: (0, nj + j)),
            pl.BlockSpec((1, tf), lambda m, j: (0, j)),
            pl.BlockSpec((1, tf), lambda m, j: (0, nj + j)),
            pl.BlockSpec((tf, D), lambda m, j: (j, 0)),
        ],
        out_specs=pl.BlockSpec((tm, D), lambda m, j: (m, 0)),
        out_shape=jax.ShapeDtypeStruct((M, D), _F32),
        scratch_shapes=[
            pltpu.VMEM((tm, D), _BF16),
            pltpu.VMEM((nj, SUBLANES, tf), _F32),
            pltpu.VMEM((nj, SUBLANES, tf), _F32),
        ],
        compiler_params=pltpu.CompilerParams(
            dimension_semantics=("arbitrary", "arbitrary"), vmem_limit_bytes=VMEM_LIMIT),
        name="ffn",
    )(x2d, g, w_up, w_up, cw, cw, cb, cb, w_down)


def _ple_kernel(x_ref, g_ref, p_ref, wg_ref, wp_ref, gf_ref, o_ref, *, final_norm, rows):
    def normed(r):
        return _rmsnorm_rows(x_ref[r * rows:(r + 1) * rows, :], g_ref[...]).astype(_BF16)

    n_chunks = x_ref.shape[0] // rows
    h = normed(0)
    for r in range(n_chunks):
        h_next = normed(r + 1) if r + 1 < n_chunks else None
        sl = slice(r * rows, (r + 1) * rows)
        gate = jax.nn.sigmoid(_dot(h, wg_ref[...]))
        x = x_ref[sl, :] + gate * _dot(p_ref[sl, :].astype(_BF16), wp_ref[...])
        if final_norm:
            x = _rmsnorm_rows(x, gf_ref[...])
        o_ref[sl, :] = x
        h = h_next


def _ple(x2d, g, p2d, w_gate, w_proj, g_final, *, final_norm, tm=512, rows=256):
    M, D = x2d.shape
    P = p2d.shape[1]
    return pl.pallas_call(
        functools.partial(_ple_kernel, final_norm=final_norm, rows=rows),
        grid=(M // tm,),
        in_specs=[
            pl.BlockSpec((tm, D), lambda m: (m, 0)),
            pl.BlockSpec((1, D), lambda m: (0, 0)),
            pl.BlockSpec((tm, P), lambda m: (m, 0)),
            pl.BlockSpec((D, D), lambda m: (0, 0)),
            pl.BlockSpec((P, D), lambda m: (0, 0)),
            pl.BlockSpec((1, D), lambda m: (0, 0)),
        ],
        out_specs=pl.BlockSpec((tm, D), lambda m: (m, 0)),
        out_shape=jax.ShapeDtypeStruct((M, D), _F32),
        compiler_params=pltpu.CompilerParams(
            dimension_semantics=("arbitrary",), vmem_limit_bytes=VMEM_LIMIT),
        name="ple",
    )(x2d, g, p2d, w_gate, w_proj, g_final)


def kernel(x, p, norm_mix_g, w_in, b_igate, b_fgate, short_conv_w, mh_norm_g, w_out,
           norm_ffn_g, w_up, ffn_conv_w, ffn_conv_b, w_down, norm_ple_g, w_ple_gate,
           w_ple_proj, final_norm_g):
    B, S, D = x.shape
    depth = w_in.shape[0]
    d_conv = short_conv_w.shape[-1]
    d_mlstm = mh_norm_g.shape[-1]
    dv = d_mlstm // N_HEADS
    d_qk = (w_in.shape[-1] - 3 * d_conv - 2 * d_mlstm - 2 * N_HEADS) // 2
    dqk = d_qk // N_HEADS
    M = B * S

    x2d = x.reshape(M, D)
    row = lambda a: a.reshape(1, -1).astype(_F32)
    colscale = jnp.concatenate([
        jnp.full((1, d_qk), dqk ** -0.5, _F32),
        jnp.ones((1, d_qk + 2 * d_mlstm), _F32)], axis=1)

    for i in range(depth):
        w = w_in[i]
        c0 = 3 * d_conv
        c1 = c0 + 2 * d_qk + 2 * d_mlstm
        w_t = jnp.swapaxes(w, 0, 1)
        w_gates = jnp.pad(w_t[c1:], ((0, LANES - 2 * N_HEADS), (0, 0))).astype(_BF16)
        gate_bias = jnp.pad(jnp.concatenate([b_igate[i], b_fgate[i]]).astype(_F32),
                            (0, LANES - 2 * N_HEADS)).reshape(1, LANES)

        g_mix = row(norm_mix_g[i])
        y_conv, w_qkvo_t = _conv_group(x2d, g_mix, w_t, short_conv_w[i].astype(_F32),
                                       [(w_t, c0, c1 - c0)], seq=S)
        qkvo, gates = _qkvo(x2d, g_mix, w_qkvo_t, colscale, w_gates)
        y_m, w_up_b, w_down_b, w_out_b, w_pg_b = _mlstm(
            qkvo, gates, gate_bias, row(mh_norm_g[i]),
            [(a, 0, a.shape[0]) for a in (w_up[i], w_down[i], w_out[i], w_ple_gate[i])],
            batch=B, seq=S, dqk=dqk, dv=dv)
        x2d = _out_proj(x2d, y_conv, y_m, w_out_b)
        x2d = _ffn(x2d, row(norm_ffn_g[i]), w_up_b, ffn_conv_w[i].astype(_F32),
                   row(ffn_conv_b[i]), w_down_b, seq=S)
        x2d = _ple(x2d, row(norm_ple_g[i]), p[i].reshape(M, -1), w_pg_b,
                   w_ple_proj[i].astype(_BF16), row(final_norm_g),
                   final_norm=(i == depth - 1))
    return x2d.reshape(B, S, D)
```

```python
import functools

import jax
import jax.numpy as jnp
from jax import lax
from jax.experimental import pallas as pl
from jax.experimental.pallas import tpu as pltpu

N_HEADS = 4
CONV_WIDTH = 3
EPS = 1e-6
LANES = 128
SUBLANES = 8
MLSTM_CHUNK = 256
W_COLS = 512
VMEM_LIMIT = 62 * 1024 * 1024

_BF16 = jnp.bfloat16
_F32 = jnp.float32


def _dot(a, b):
    return jnp.dot(a, b, preferred_element_type=_F32)


def _dot_nt(a, bt):
    return lax.dot_general(a, bt, (((1,), (1,)), ((), ())), preferred_element_type=_F32)


def _rmsnorm_rows(x, g):
    ms = jnp.mean(x * x, axis=-1, keepdims=True)
    return x * lax.rsqrt(ms + EPS) * g


def _causal_conv3(z, prev, cw):
    n = z.shape[0]
    ext = jnp.concatenate([prev, z], axis=0)
    z2 = ext[SUBLANES - 2:SUBLANES - 2 + n, :]
    z1 = ext[SUBLANES - 1:SUBLANES - 1 + n, :]
    return z2 * cw[0:1, :] + z1 * cw[1:2, :] + z * cw[2:3, :]


def _load_halo(halo_ref, j, first_tile):
    @pl.when(first_tile)
    def _():
        halo_ref[j] = jnp.zeros(halo_ref.shape[1:], _F32)

    return halo_ref[j]


def _cast_plan(items, n_steps, step_index):
    in_specs, out_specs, shapes = [], [], []
    for a, row0, n_rows in items:
        slab = n_rows // n_steps
        assert slab * n_steps == n_rows and slab % (2 * SUBLANES) == 0 and row0 % slab == 0
        cols = a.shape[1]
        in_specs.append(pl.BlockSpec(
            (slab, cols), lambda *idx, b0=row0 // slab: (b0 + step_index(*idx), 0)))
        out_specs.append(pl.BlockSpec((slab, cols), lambda *idx: (step_index(*idx), 0)))
        shapes.append(jax.ShapeDtypeStruct((n_rows, cols), _BF16))
    return in_specs, out_specs, shapes


def _cast_slabs(refs):
    half = len(refs) // 2
    for src, dst in zip(refs[:half], refs[half:]):
        dst[...] = src[...].astype(dst.dtype)


def _conv_group_kernel(x_ref, g_ref, wb_ref, wc_ref, wu_ref, cw_ref, *rest,
                       tiles_per_seq, rows, n_cast):
    cast_in, (o_ref, *cast_out) = rest[:n_cast], rest[n_cast:2 * n_cast + 1]
    h_ref, halo_ref, w_ref = rest[2 * n_cast + 1:]
    m = pl.program_id(0)
    j = pl.program_id(1)
    tm = x_ref.shape[0]

    _cast_slabs(list(cast_in) + list(cast_out))

    @pl.when(j == 0)
    def _():
        h_ref[...] = _rmsnorm_rows(x_ref[...], g_ref[...]).astype(_BF16)

    @pl.when(m == 0)
    def _():
        w_ref[3 * j] = wb_ref[...].astype(_BF16)
        w_ref[3 * j + 1] = wc_ref[...].astype(_BF16)
        w_ref[3 * j + 2] = wu_ref[...].astype(_BF16)

    def proj(r):
        h = h_ref[r * rows:(r + 1) * rows, :]
        return (_dot_nt(h, w_ref[3 * j]), _dot_nt(h, w_ref[3 * j + 1]),
                _dot_nt(h, w_ref[3 * j + 2]))

    prev = _load_halo(halo_ref, j, m % tiles_per_seq == 0)
    n_chunks = tm // rows
    cur = proj(0)
    for r in range(n_chunks):
        nxt = proj(r + 1) if r + 1 < n_chunks else None
        gb, gc, u = cur
        z = gc * u
        y = _causal_conv3(z, prev, cw_ref[...])
        o_ref[r * rows:(r + 1) * rows, :] = (gb * y).astype(o_ref.dtype)
        prev = z[rows - SUBLANES:, :]
        cur = nxt
    halo_ref[j] = prev


def _conv_group(x2d, g, w_in_t, cw, cast_items, *, seq, tm=1024, tc=256, rows=256):
    M, D = x2d.shape
    C = cw.shape[1]
    nj = C // tc
    n_steps = (M // tm) * nj
    cast_in, cast_out, cast_shapes = _cast_plan(cast_items, n_steps, lambda m, j: m * nj + j)
    wtile = lambda m, j: jnp.where(m == 0, j, nj - 1)
    return pl.pallas_call(
        functools.partial(_conv_group_kernel, tiles_per_seq=seq // tm, rows=rows,
                          n_cast=len(cast_items)),
        grid=(M // tm, nj),
        in_specs=[
            pl.BlockSpec((tm, D), lambda m, j: (m, 0)),
            pl.BlockSpec((1, D), lambda m, j: (0, 0)),
            pl.BlockSpec((tc, D), lambda m, j: (wtile(m, j), 0)),
            pl.BlockSpec((tc, D), lambda m, j: (nj + wtile(m, j), 0)),
            pl.BlockSpec((tc, D), lambda m, j: (2 * nj + wtile(m, j), 0)),
            pl.BlockSpec((CONV_WIDTH, tc), lambda m, j: (0, j)),
        ] + cast_in,
        out_specs=[pl.BlockSpec((tm, tc), lambda m, j: (m, j))] + cast_out,
        out_shape=[jax.ShapeDtypeStruct((M, C), _BF16)] + cast_shapes,
        scratch_shapes=[
            pltpu.VMEM((tm, D), _BF16),
            pltpu.VMEM((nj, SUBLANES, tc), _F32),
            pltpu.VMEM((3 * nj, tc, D), _BF16),
        ],
        compiler_params=pltpu.CompilerParams(
            dimension_semantics=("arbitrary", "arbitrary"), vmem_limit_bytes=VMEM_LIMIT),
        name="conv_group",
    )(x2d, g, w_in_t, w_in_t, w_in_t, cw, *[a for a, _, _ in cast_items])


def _qkvo_kernel(x_ref, g_ref, w_ref, cs_ref, wgate_ref, o_ref, gate_ref, h_ref):
    j = pl.program_id(1)

    @pl.when(j == 0)
    def _():
        h = _rmsnorm_rows(x_ref[...], g_ref[...]).astype(_BF16)
        h_ref[...] = h
        gate_ref[...] = _dot_nt(h, wgate_ref[...])

    tn = o_ref.shape[1]
    w = w_ref[pl.ds(pl.multiple_of(j * tn, tn), tn), :]
    o_ref[...] = (_dot_nt(h_ref[...], w) * cs_ref[...]).astype(o_ref.dtype)


def _qkvo(x2d, g, w_t, colscale, wgate, *, tm=1024, tn=1024):
    M, D = x2d.shape
    N = colscale.shape[1]
    assert w_t.shape == (N, D) and N % tn == 0
    return pl.pallas_call(
        _qkvo_kernel,
        grid=(M // tm, N // tn),
        in_specs=[
            pl.BlockSpec((tm, D), lambda m, j: (m, 0)),
            pl.BlockSpec((1, D), lambda m, j: (0, 0)),
            pl.BlockSpec((N, D), lambda m, j: (0, 0), pipeline_mode=pl.Buffered(1)),
            pl.BlockSpec((1, tn), lambda m, j: (0, j)),
            pl.BlockSpec((LANES, D), lambda m, j: (0, 0)),
        ],
        out_specs=[
            pl.BlockSpec((tm, tn), lambda m, j: (m, j)),
            pl.BlockSpec((tm, LANES), lambda m, j: (m, 0)),
        ],
        out_shape=[
            jax.ShapeDtypeStruct((M, N), _BF16),
            jax.ShapeDtypeStruct((M, LANES), _F32),
        ],
        scratch_shapes=[pltpu.VMEM((tm, D), _BF16)],
        compiler_params=pltpu.CompilerParams(
            dimension_semantics=("arbitrary", "arbitrary"), vmem_limit_bytes=VMEM_LIMIT),
        name="qkvo",
    )(x2d, g, w_t, colscale, wgate)


def _exact_cumsum_rows(tril, a):
    hi = a.astype(_BF16)
    r1 = a - hi.astype(_F32)
    mid = r1.astype(_BF16)
    lo = (r1 - mid.astype(_F32)).astype(_BF16)
    return _dot(tril, hi) + _dot(tril, mid) + _dot(tril, lo)


def _mlstm_out_kernel(q_ref, k_ref, v_ref, o_ref, gate_ref, bias_ref, gn_ref,
                      x_ref, yc_ref, wo_ref, *rest, dqk, dv, n_cast, nc):
    cast_in, (x1_ref, *cast_out) = rest[:n_cast], rest[n_cast:2 * n_cast + 1]
    c_ref, n_ref, m_ref, ym_ref = rest[2 * n_cast + 1:]
    t = pl.program_id(0)
    bb, L = q_ref.shape[0], q_ref.shape[1]
    kc = yc_ref.shape[2]

    _cast_slabs(list(cast_in) + list(cast_out))

    @pl.when(t % nc == 0)
    def _():
        c_ref[...] = jnp.zeros(c_ref.shape, _F32)
        n_ref[...] = jnp.zeros(n_ref.shape, _F32)
        m_ref[...] = jnp.zeros(m_ref.shape, _F32)

    @pl.when(t == 0)
    def _():
        ym_ref[...] = jnp.zeros(ym_ref.shape, ym_ref.dtype)

    def out_proj(bi):
        acc = _dot(yc_ref[bi], wo_ref[0:kc, :]) + _dot(ym_ref[bi], wo_ref[kc:, :])
        x1_ref[bi] = x_ref[bi] + acc

    row = lax.broadcasted_iota(jnp.int32, (L, L), 0)
    col = lax.broadcasted_iota(jnp.int32, (L, L), 1)
    causal = col <= row
    tril = causal.astype(_BF16)

    chains = [(bi, h) for bi in range(bb) for h in range(N_HEADS)]
    pre = {}
    for bi in range(bb):
        gates = gate_ref[bi] + bias_ref[...]
        logf = jnp.minimum(gates, 0.0) - jnp.log1p(jnp.exp(-jnp.abs(gates)))
        bcum = _exact_cumsum_rows(tril, logf)
        pre[bi] = (gates, bcum, gates.T, bcum.T)

    stab = {}
    for bi, h in chains:
        gates, bcum, gates_t, bcum_t = pre[bi]
        st = bi * N_HEADS + h
        b_col = bcum[:, N_HEADS + h:N_HEADS + h + 1]
        r_row = gates_t[h:h + 1, :] - bcum_t[N_HEADS + h:N_HEADS + h + 1, :]
        dlog = jnp.where(causal, b_col + r_row, -jnp.inf)
        inter = b_col + m_ref[st:st + 1, 0:1]
        m_t = jnp.maximum(inter, jnp.max(dlog, axis=-1, keepdims=True))
        stab[bi, h] = (b_col, dlog, inter, m_t)

    for bi in range(0, bb, 2):
        out_proj(bi)

    out = {}
    for bi, h in chains:
        st = bi * N_HEADS + h
        b_col, dlog, inter, m_t = stab[bi, h]
        dw = jnp.exp(dlog - m_t)
        a_inter = jnp.exp(inter - m_t)
        q = q_ref[bi, :, h * dqk:(h + 1) * dqk]
        k = k_ref[bi, :, h * dqk:(h + 1) * dqk]
        v = v_ref[bi, :, h * dv:(h + 1) * dv]
        s = _dot_nt(q, k) * dw
        num = a_inter * _dot(q, c_ref[st].astype(_BF16)) + _dot(s.astype(_BF16), v)
        qn = jnp.sum(q.astype(_F32) * n_ref[st:st + 1, :], axis=-1, keepdims=True)
        den = a_inter * qn + jnp.sum(s, axis=-1, keepdims=True)
        den = jnp.maximum(jnp.abs(den), jnp.exp(-m_t))
        out[bi, h] = (num / den, a_inter)

    for bi in range(1, bb, 2):
        out_proj(bi)

    for bi, h in chains:
        hm, _ = out[bi, h]
        hm = hm * lax.rsqrt(jnp.mean(hm * hm, axis=-1, keepdims=True) + EPS)
        hm = hm * gn_ref[:, h * dv:(h + 1) * dv]
        og = jax.nn.sigmoid(o_ref[bi, :, h * dv:(h + 1) * dv].astype(_F32))
        ym_ref[bi, :, h * dv:(h + 1) * dv] = (og * hm).astype(ym_ref.dtype)

    for bi, h in chains:
        st = bi * N_HEADS + h
        gates = pre[bi][0]
        b_col, _, _, m_t = stab[bi, h]
        a_inter = out[bi, h][1]
        k = k_ref[bi, :, h * dqk:(h + 1) * dqk]
        v = v_ref[bi, :, h * dv:(h + 1) * dv]
        b_tot = b_col[L - 1:L, :]
        m_new = m_t[L - 1:L, :]
        a_prev = a_inter[L - 1:L, :]
        w_col = jnp.exp(b_tot - b_col + gates[:, h:h + 1] - m_new)
        kw = k.astype(_F32) * w_col
        c_ref[st] = a_prev * c_ref[st] + _dot(kw.T.astype(_BF16), v)
        n_ref[st:st + 1, :] = a_prev * n_ref[st:st + 1, :] + jnp.sum(kw, axis=0, keepdims=True)
        m_ref[st:st + 1, :] = jnp.broadcast_to(m_new, (1, LANES))


def _mlstm_out(qkvo, gates, gate_bias, gnorm, x2d, y_conv, w_out, cast_items, *,
               batch, seq, dqk, dv, bb=2):
    L = MLSTM_CHUNK
    nc = seq // L
    D = x2d.shape[1]
    dq_all = N_HEADS * dqk
    dv_all = N_HEADS * dv
    kc = y_conv.shape[1]
    assert dv_all == 2 * dq_all and batch % bb == 0 and bb * N_HEADS <= SUBLANES
    assert w_out.shape == (kc + dv_all, D)
    n_blocks = (batch // bb) * nc
    qkvo3 = qkvo.reshape(batch, seq, -1)
    gates3 = gates.reshape(batch, seq, LANES)
    x3 = x2d.reshape(batch, seq, D)
    yc3 = y_conv.reshape(batch, seq, kc)

    def cur(col):
        def index(t):
            tb = jnp.minimum(t, n_blocks - 1)
            return tb // nc, tb % nc, col
        return index

    def prev(t):
        tb = jnp.maximum(t - 1, 0)
        return tb // nc, tb % nc, 0

    cast_in, cast_out, cast_shapes = _cast_plan(
        cast_items, n_blocks, lambda t: jnp.minimum(t, n_blocks - 1))
    x1, *cast = pl.pallas_call(
        functools.partial(_mlstm_out_kernel, dqk=dqk, dv=dv, n_cast=len(cast_items), nc=nc),
        grid=(n_blocks + 1,),
        in_specs=[
            pl.BlockSpec((bb, L, dq_all), cur(0)),
            pl.BlockSpec((bb, L, dq_all), cur(1)),
            pl.BlockSpec((bb, L, dv_all), cur(1)),
            pl.BlockSpec((bb, L, dv_all), cur(2)),
            pl.BlockSpec((bb, L, LANES), cur(0)),
            pl.BlockSpec((1, LANES), lambda t: (0, 0)),
            pl.BlockSpec((1, dv_all), lambda t: (0, 0)),
            pl.BlockSpec((bb, L, D), prev),
            pl.BlockSpec((bb, L, kc), prev),
            pl.BlockSpec((kc + dv_all, D), lambda t: (0, 0), pipeline_mode=pl.Buffered(1)),
        ] + cast_in,
        out_specs=[pl.BlockSpec((bb, L, D), prev)] + cast_out,
        out_shape=[jax.ShapeDtypeStruct((batch, seq, D), _F32)] + cast_shapes,
        scratch_shapes=[
            pltpu.VMEM((bb * N_HEADS, dqk, dv), _F32),
            pltpu.VMEM((SUBLANES, dqk), _F32),
            pltpu.VMEM((SUBLANES, LANES), _F32),
            pltpu.VMEM((bb, L, dv_all), _BF16),
        ],
        compiler_params=pltpu.CompilerParams(
            dimension_semantics=("arbitrary",), vmem_limit_bytes=VMEM_LIMIT),
        name="mlstm_out",
    )(qkvo3, qkvo3, qkvo3, qkvo3, gates3, gate_bias, gnorm, x3, yc3, w_out,
      *[a for a, _, _ in cast_items])
    return (x1.reshape(batch * seq, D), *cast)


def _ffn_kernel(x_ref, g_ref, wg_ref, wu_ref, cwg_ref, cwu_ref, bg_ref, bu_ref, wd_ref,
                o_ref, h_ref, halog_ref, halou_ref, *, tiles_per_seq, rows):
    m = pl.program_id(0)
    j = pl.program_id(1)
    tm = x_ref.shape[0]

    @pl.when(j == 0)
    def _():
        x = x_ref[...]
        h_ref[...] = _rmsnorm_rows(x, g_ref[...]).astype(_BF16)
        o_ref[...] = x

    first = m % tiles_per_seq == 0
    prev_g = _load_halo(halog_ref, j, first)
    prev_u = _load_halo(halou_ref, j, first)

    def up_proj(r):
        h = h_ref[r * rows:(r + 1) * rows, :]
        return _dot(h, wg_ref[...]), _dot(h, wu_ref[...])

    n_chunks = tm // rows
    cur = up_proj(0)
    for r in range(n_chunks):
        nxt = up_proj(r + 1) if r + 1 < n_chunks else None
        zg, zu = cur
        gate = _causal_conv3(zg, prev_g, cwg_ref[...]) + bg_ref[...]
        up = _causal_conv3(zu, prev_u, cwu_ref[...]) + bu_ref[...]
        act = (jax.nn.silu(gate) * up).astype(_BF16)
        o_ref[r * rows:(r + 1) * rows, :] += _dot(act, wd_ref[...])
        prev_g = zg[rows - SUBLANES:, :]
        prev_u = zu[rows - SUBLANES:, :]
        cur = nxt
    halog_ref[j] = prev_g
    halou_ref[j] = prev_u


def _ffn(x2d, g, w_up, cw, cb, w_down, *, seq, tm=1024, tf=512, rows=512):
    M, D = x2d.shape
    F = w_down.shape[0]
    nj = F // tf
    return pl.pallas_call(
        functools.partial(_ffn_kernel, tiles_per_seq=seq // tm, rows=rows),
        grid=(M // tm, nj),
        in_specs=[
            pl.BlockSpec((tm, D), lambda m, j: (m, 0)),
            pl.BlockSpec((1, D), lambda m, j: (0, 0)),
            pl.BlockSpec((D, tf), lambda m, j: (0, j)),
            pl.BlockSpec((D, tf), lambda m, j: (0, nj + j)),
            pl.BlockSpec((CONV_WIDTH, tf), lambda m, j: (0, j)),
            pl.BlockSpec((CONV_WIDTH, tf), lambda m, j: (0, nj + j)),
            pl.BlockSpec((1, tf), lambda m, j: (0, j)),
            pl.BlockSpec((1, tf), lambda m, j: (0, nj + j)),
            pl.BlockSpec((tf, D), lambda m, j: (j, 0)),
        ],
        out_specs=pl.BlockSpec((tm, D), lambda m, j: (m, 0)),
        out_shape=jax.ShapeDtypeStruct((M, D), _F32),
        scratch_shapes=[
            pltpu.VMEM((tm, D), _BF16),
            pltpu.VMEM((nj, SUBLANES, tf), _F32),
            pltpu.VMEM((nj, SUBLANES, tf), _F32),
        ],
        compiler_params=pltpu.CompilerParams(
            dimension_semantics=("arbitrary", "arbitrary"), vmem_limit_bytes=VMEM_LIMIT),
        name="ffn",
    )(x2d, g, w_up, w_up, cw, cw, cb, cb, w_down)


def _ple_kernel(x_ref, g_ref, p_ref, wg_ref, wp_ref, gf_ref, o_ref, *, final_norm, rows):
    def normed(r):
        return _rmsnorm_rows(x_ref[r * rows:(r + 1) * rows, :], g_ref[...]).astype(_BF16)

    n_chunks = x_ref.shape[0] // rows
    h = normed(0)
    for r in range(n_chunks):
        h_next = normed(r + 1) if r + 1 < n_chunks else None
        sl = slice(r * rows, (r + 1) * rows)
        gate = jax.nn.sigmoid(_dot(h, wg_ref[...]))
        x = x_ref[sl, :] + gate * _dot(p_ref[sl, :].astype(_BF16), wp_ref[...])
        if final_norm:
            x = _rmsnorm_rows(x, gf_ref[...])
        o_ref[sl, :] = x
        h = h_next


def _ple(x2d, g, p2d, w_gate, w_proj, g_final, *, final_norm, tm=512, rows=256):
    M, D = x2d.shape
    P = p2d.shape[1]
    return pl.pallas_call(
        functools.partial(_ple_kernel, final_norm=final_norm, rows=rows),
        grid=(M // tm,),
        in_specs=[
            pl.BlockSpec((tm, D), lambda m: (m, 0)),
            pl.BlockSpec((1, D), lambda m: (0, 0)),
            pl.BlockSpec((tm, P), lambda m: (m, 0)),
            pl.BlockSpec((D, D), lambda m: (0, 0)),
            pl.BlockSpec((P, D), lambda m: (0, 0)),
            pl.BlockSpec((1, D), lambda m: (0, 0)),
        ],
        out_specs=pl.BlockSpec((tm, D), lambda m: (m, 0)),
        out_shape=jax.ShapeDtypeStruct((M, D), _F32),
        compiler_params=pltpu.CompilerParams(
            dimension_semantics=("arbitrary",), vmem_limit_bytes=VMEM_LIMIT),
        name="ple",
    )(x2d, g, p2d, w_gate, w_proj, g_final)


def kernel(x, p, norm_mix_g, w_in, b_igate, b_fgate, short_conv_w, mh_norm_g, w_out,
           norm_ffn_g, w_up, ffn_conv_w, ffn_conv_b, w_down, norm_ple_g, w_ple_gate,
           w_ple_proj, final_norm_g):
    B, S, D = x.shape
    depth = w_in.shape[0]
    d_conv = short_conv_w.shape[-1]
    d_mlstm = mh_norm_g.shape[-1]
    dv = d_mlstm // N_HEADS
    d_qk = (w_in.shape[-1] - 3 * d_conv - 2 * d_mlstm - 2 * N_HEADS) // 2
    dqk = d_qk // N_HEADS
    M = B * S

    x2d = x.reshape(M, D)
    row = lambda a: a.reshape(1, -1).astype(_F32)
    colscale = jnp.concatenate([
        jnp.full((1, d_qk), dqk ** -0.5, _F32),
        jnp.ones((1, d_qk + 2 * d_mlstm), _F32)], axis=1)

    for i in range(depth):
        w = w_in[i]
        c0 = 3 * d_conv
        c1 = c0 + 2 * d_qk + 2 * d_mlstm
        w_t = jnp.swapaxes(w, 0, 1)
        w_gates = jnp.pad(w_t[c1:], ((0, LANES - 2 * N_HEADS), (0, 0))).astype(_BF16)
        gate_bias = jnp.pad(jnp.concatenate([b_igate[i], b_fgate[i]]).astype(_F32),
                            (0, LANES - 2 * N_HEADS)).reshape(1, LANES)

        g_mix = row(norm_mix_g[i])
        whole = lambda a: (a, 0, a.shape[0])
        y_conv, w_qkvo_t, w_up_b, w_out_b = _conv_group(
            x2d, g_mix, w_t, short_conv_w[i].astype(_F32),
            [(w_t, c0, c1 - c0), whole(w_up[i]), whole(w_out[i])], seq=S)
        qkvo, gates = _qkvo(x2d, g_mix, w_qkvo_t, colscale, w_gates)
        x2d, w_down_b, w_pg_b = _mlstm_out(
            qkvo, gates, gate_bias, row(mh_norm_g[i]), x2d, y_conv, w_out_b,
            [whole(w_down[i]), whole(w_ple_gate[i])], batch=B, seq=S, dqk=dqk, dv=dv)
        x2d = _ffn(x2d, row(norm_ffn_g[i]), w_up_b, ffn_conv_w[i].astype(_F32),
                   row(ffn_conv_b[i]), w_down_b, seq=S)
        x2d = _ple(x2d, row(norm_ple_g[i]), p[i].reshape(M, -1), w_pg_b,
                   w_ple_proj[i].astype(_BF16), row(final_norm_g),
                   final_norm=(i == depth - 1))
    return x2d.reshape(B, S, D)
```

```python
import functools

import jax
import jax.numpy as jnp
from jax import lax
from jax.experimental import pallas as pl
from jax.experimental.pallas import tpu as pltpu

N_HEADS = 4
CONV_WIDTH = 3
EPS = 1e-6
LANES = 128
SUBLANES = 8
MLSTM_CHUNK = 256
W_COLS = 512
VMEM_LIMIT = 62 * 1024 * 1024

_BF16 = jnp.bfloat16
_F32 = jnp.float32


def _dot(a, b):
    return jnp.dot(a, b, preferred_element_type=_F32)


def _dot_nt(a, bt):
    return lax.dot_general(a, bt, (((1,), (1,)), ((), ())), preferred_element_type=_F32)


def _rmsnorm_rows(x, g):
    ms = jnp.mean(x * x, axis=-1, keepdims=True)
    return x * lax.rsqrt(ms + EPS) * g


def _causal_conv3(z, prev, cw):
    n = z.shape[0]
    ext = jnp.concatenate([prev, z], axis=0)
    z2 = ext[SUBLANES - 2:SUBLANES - 2 + n, :]
    z1 = ext[SUBLANES - 1:SUBLANES - 1 + n, :]
    return z2 * cw[0:1, :] + z1 * cw[1:2, :] + z * cw[2:3, :]


def _load_halo(halo_ref, j, first_tile):
    @pl.when(first_tile)
    def _():
        halo_ref[j] = jnp.zeros(halo_ref.shape[1:], _F32)

    return halo_ref[j]


def _cast_plan(items, n_row, n_col, step_index):
    in_specs, out_specs, shapes = [], [], []
    for a, row0, n_rows in items:
        slab = n_rows // n_row
        cols = a.shape[1]
        width = cols // n_col
        assert slab * n_row == n_rows and slab % (2 * SUBLANES) == 0 and row0 % slab == 0
        assert width * n_col == cols and width % LANES == 0

        def in_index(*idx, b0=row0 // slab):
            r, c = step_index(*idx)
            return b0 + r, c

        in_specs.append(pl.BlockSpec((slab, width), in_index))
        out_specs.append(pl.BlockSpec((slab, width), lambda *idx: step_index(*idx)))
        shapes.append(jax.ShapeDtypeStruct((n_rows, cols), _BF16))
    return in_specs, out_specs, shapes


def _cast_slabs(refs):
    half = len(refs) // 2
    for src, dst in zip(refs[:half], refs[half:]):
        dst[...] = src[...].astype(dst.dtype)


def _conv_group_kernel(x_ref, g_ref, wb_ref, wc_ref, wu_ref, cw_ref, *rest,
                       tiles_per_seq, rows, n_cast):
    cast_in, (o_ref, *cast_out) = rest[:n_cast], rest[n_cast:2 * n_cast + 1]
    h_ref, halo_ref, w_ref = rest[2 * n_cast + 1:]
    m = pl.program_id(0)
    j = pl.program_id(1)
    tm = x_ref.shape[0]

    _cast_slabs(list(cast_in) + list(cast_out))

    @pl.when(j == 0)
    def _():
        h_ref[...] = _rmsnorm_rows(x_ref[...], g_ref[...]).astype(_BF16)

    @pl.when(m == 0)
    def _():
        w_ref[3 * j] = wb_ref[...].astype(_BF16)
        w_ref[3 * j + 1] = wc_ref[...].astype(_BF16)
        w_ref[3 * j + 2] = wu_ref[...].astype(_BF16)

    def proj(r):
        h = h_ref[r * rows:(r + 1) * rows, :]
        return (_dot_nt(h, w_ref[3 * j]), _dot_nt(h, w_ref[3 * j + 1]),
                _dot_nt(h, w_ref[3 * j + 2]))

    prev = _load_halo(halo_ref, j, m % tiles_per_seq == 0)
    n_chunks = tm // rows
    cur = proj(0)
    for r in range(n_chunks):
        nxt = proj(r + 1) if r + 1 < n_chunks else None
        gb, gc, u = cur
        z = gc * u
        y = _causal_conv3(z, prev, cw_ref[...])
        o_ref[r * rows:(r + 1) * rows, :] = (gb * y).astype(o_ref.dtype)
        prev = z[rows - SUBLANES:, :]
        cur = nxt
    halo_ref[j] = prev


def _conv_group(x2d, g, w_in_t, cw, cast_items, *, seq, tm=1024, tc=256, rows=256):
    M, D = x2d.shape
    C = cw.shape[1]
    nj = C // tc
    n_steps = (M // tm) * nj
    cast_in, cast_out, cast_shapes = _cast_plan(
        cast_items, n_steps, 1, lambda m, j: (m * nj + j, 0))
    wtile = lambda m, j: jnp.where(m == 0, j, nj - 1)
    return pl.pallas_call(
        functools.partial(_conv_group_kernel, tiles_per_seq=seq // tm, rows=rows,
                          n_cast=len(cast_items)),
        grid=(M // tm, nj),
        in_specs=[
            pl.BlockSpec((tm, D), lambda m, j: (m, 0)),
            pl.BlockSpec((1, D), lambda m, j: (0, 0)),
            pl.BlockSpec((tc, D), lambda m, j: (wtile(m, j), 0)),
            pl.BlockSpec((tc, D), lambda m, j: (nj + wtile(m, j), 0)),
            pl.BlockSpec((tc, D), lambda m, j: (2 * nj + wtile(m, j), 0)),
            pl.BlockSpec((CONV_WIDTH, tc), lambda m, j: (0, j)),
        ] + cast_in,
        out_specs=[pl.BlockSpec((tm, tc), lambda m, j: (m, j))] + cast_out,
        out_shape=[jax.ShapeDtypeStruct((M, C), _BF16)] + cast_shapes,
        scratch_shapes=[
            pltpu.VMEM((tm, D), _BF16),
            pltpu.VMEM((nj, SUBLANES, tc), _F32),
            pltpu.VMEM((3 * nj, tc, D), _BF16),
        ],
        compiler_params=pltpu.CompilerParams(
            dimension_semantics=("arbitrary", "arbitrary"), vmem_limit_bytes=VMEM_LIMIT),
        name="conv_group",
    )(x2d, g, w_in_t, w_in_t, w_in_t, cw, *[a for a, _, _ in cast_items])


def _qkvo_kernel(x_ref, g_ref, w_ref, cs_ref, wgate_ref, *rest, n_cast):
    cast_in, (o_ref, gate_ref, *cast_out) = rest[:n_cast], rest[n_cast:2 * n_cast + 2]
    (h_ref,) = rest[2 * n_cast + 2:]
    j = pl.program_id(1)

    _cast_slabs(list(cast_in) + list(cast_out))

    @pl.when(j == 0)
    def _():
        h = _rmsnorm_rows(x_ref[...], g_ref[...]).astype(_BF16)
        h_ref[...] = h
        gate_ref[...] = _dot_nt(h, wgate_ref[...])

    tn = o_ref.shape[1]
    w = w_ref[pl.ds(pl.multiple_of(j * tn, tn), tn), :]
    o_ref[...] = (_dot_nt(h_ref[...], w) * cs_ref[...]).astype(o_ref.dtype)


def _qkvo(x2d, g, w_t, colscale, wgate, cast_items, *, tm=1024, tn=768):
    M, D = x2d.shape
    N = colscale.shape[1]
    assert w_t.shape == (N, D) and N % tn == 0
    cast_in, cast_out, cast_shapes = _cast_plan(
        cast_items, M // tm, N // tn, lambda m, j: (m, j))
    return pl.pallas_call(
        functools.partial(_qkvo_kernel, n_cast=len(cast_items)),
        grid=(M // tm, N // tn),
        in_specs=[
            pl.BlockSpec((tm, D), lambda m, j: (m, 0)),
            pl.BlockSpec((1, D), lambda m, j: (0, 0)),
            pl.BlockSpec((N, D), lambda m, j: (0, 0), pipeline_mode=pl.Buffered(1)),
            pl.BlockSpec((1, tn), lambda m, j: (0, j)),
            pl.BlockSpec((LANES, D), lambda m, j: (0, 0)),
        ] + cast_in,
        out_specs=[
            pl.BlockSpec((tm, tn), lambda m, j: (m, j)),
            pl.BlockSpec((tm, LANES), lambda m, j: (m, 0)),
        ] + cast_out,
        out_shape=[
            jax.ShapeDtypeStruct((M, N), _BF16),
            jax.ShapeDtypeStruct((M, LANES), _F32),
        ] + cast_shapes,
        scratch_shapes=[pltpu.VMEM((tm, D), _BF16)],
        compiler_params=pltpu.CompilerParams(
            dimension_semantics=("arbitrary", "arbitrary"), vmem_limit_bytes=VMEM_LIMIT),
        name="qkvo",
    )(x2d, g, w_t, colscale, wgate, *[a for a, _, _ in cast_items])


def _exact_cumsum_rows(tril, a):
    hi = a.astype(_BF16)
    r1 = a - hi.astype(_F32)
    mid = r1.astype(_BF16)
    lo = (r1 - mid.astype(_F32)).astype(_BF16)
    return _dot(tril, hi) + _dot(tril, mid) + _dot(tril, lo)


def _mlstm_out_kernel(q_ref, k_ref, v_ref, o_ref, gate_ref, bias_ref, gn_ref,
                      x_ref, yc_ref, wo_ref, *rest, dqk, dv, n_cast, nc):
    cast_in, (x1_ref, *cast_out) = rest[:n_cast], rest[n_cast:2 * n_cast + 1]
    c_ref, n_ref, m_ref, ym_ref = rest[2 * n_cast + 1:]
    t = pl.program_id(0)
    bb, L = q_ref.shape[0], q_ref.shape[1]
    kc = yc_ref.shape[2]

    _cast_slabs(list(cast_in) + list(cast_out))

    @pl.when(t % nc == 0)
    def _():
        c_ref[...] = jnp.zeros(c_ref.shape, _F32)
        n_ref[...] = jnp.zeros(n_ref.shape, _F32)
        m_ref[...] = jnp.zeros(m_ref.shape, _F32)

    @pl.when(t == 0)
    def _():
        ym_ref[...] = jnp.zeros(ym_ref.shape, ym_ref.dtype)

    def out_proj(bi):
        acc = _dot(yc_ref[bi], wo_ref[0:kc, :]) + _dot(ym_ref[bi], wo_ref[kc:, :])
        x1_ref[bi] = x_ref[bi] + acc

    row = lax.broadcasted_iota(jnp.int32, (L, L), 0)
    col = lax.broadcasted_iota(jnp.int32, (L, L), 1)
    causal = col <= row
    tril = causal.astype(_BF16)

    chains = [(bi, h) for bi in range(bb) for h in range(N_HEADS)]
    pre = {}
    for bi in range(bb):
        gates = gate_ref[bi] + bias_ref[...]
        logf = jnp.minimum(gates, 0.0) - jnp.log1p(jnp.exp(-jnp.abs(gates)))
        bcum = _exact_cumsum_rows(tril, logf)
        pre[bi] = (gates, bcum, gates.T, bcum.T)

    stab = {}
    for bi, h in chains:
        gates, bcum, gates_t, bcum_t = pre[bi]
        st = bi * N_HEADS + h
        b_col = bcum[:, N_HEADS + h:N_HEADS + h + 1]
        r_row = gates_t[h:h + 1, :] - bcum_t[N_HEADS + h:N_HEADS + h + 1, :]
        dlog = jnp.where(causal, b_col + r_row, -jnp.inf)
        inter = b_col + m_ref[st:st + 1, 0:1]
        m_t = jnp.maximum(inter, jnp.max(dlog, axis=-1, keepdims=True))
        stab[bi, h] = (b_col, dlog, inter, m_t)

    for bi in range(0, bb, 2):
        out_proj(bi)

    out = {}
    for bi, h in chains:
        st = bi * N_HEADS + h
        b_col, dlog, inter, m_t = stab[bi, h]
        dw = jnp.exp(dlog - m_t)
        a_inter = jnp.exp(inter - m_t)
        q = q_ref[bi, :, h * dqk:(h + 1) * dqk]
        k = k_ref[bi, :, h * dqk:(h + 1) * dqk]
        v = v_ref[bi, :, h * dv:(h + 1) * dv]
        s = _dot_nt(q, k) * dw
        num = a_inter * _dot(q, c_ref[st].astype(_BF16)) + _dot(s.astype(_BF16), v)
        qn = jnp.sum(q.astype(_F32) * n_ref[st:st + 1, :], axis=-1, keepdims=True)
        den = a_inter * qn + jnp.sum(s, axis=-1, keepdims=True)
        den = jnp.maximum(jnp.abs(den), jnp.exp(-m_t))
        out[bi, h] = (num / den, a_inter)

    for bi in range(1, bb, 2):
        out_proj(bi)

    for bi, h in chains:
        hm, _ = out[bi, h]
        hm = hm * lax.rsqrt(jnp.mean(hm * hm, axis=-1, keepdims=True) + EPS)
        hm = hm * gn_ref[:, h * dv:(h + 1) * dv]
        og = jax.nn.sigmoid(o_ref[bi, :, h * dv:(h + 1) * dv].astype(_F32))
        ym_ref[bi, :, h * dv:(h + 1) * dv] = (og * hm).astype(ym_ref.dtype)

    for bi, h in chains:
        st = bi * N_HEADS + h
        gates = pre[bi][0]
        b_col, _, _, m_t = stab[bi, h]
        a_inter = out[bi, h][1]
        k = k_ref[bi, :, h * dqk:(h + 1) * dqk]
        v = v_ref[bi, :, h * dv:(h + 1) * dv]
        b_tot = b_col[L - 1:L, :]
        m_new = m_t[L - 1:L, :]
        a_prev = a_inter[L - 1:L, :]
        w_col = jnp.exp(b_tot - b_col + gates[:, h:h + 1] - m_new)
        kw = k.astype(_F32) * w_col
        c_ref[st] = a_prev * c_ref[st] + _dot(kw.T.astype(_BF16), v)
        n_ref[st:st + 1, :] = a_prev * n_ref[st:st + 1, :] + jnp.sum(kw, axis=0, keepdims=True)
        m_ref[st:st + 1, :] = jnp.broadcast_to(m_new, (1, LANES))


def _mlstm_out(qkvo, gates, gate_bias, gnorm, x2d, y_conv, w_out, cast_items, *,
               batch, seq, dqk, dv, bb=2):
    L = MLSTM_CHUNK
    nc = seq // L
    D = x2d.shape[1]
    dq_all = N_HEADS * dqk
    dv_all = N_HEADS * dv
    kc = y_conv.shape[1]
    assert dv_all == 2 * dq_all and batch % bb == 0 and bb * N_HEADS <= SUBLANES
    assert w_out.shape == (kc + dv_all, D)
    n_blocks = (batch // bb) * nc
    qkvo3 = qkvo.reshape(batch, seq, -1)
    gates3 = gates.reshape(batch, seq, LANES)
    x3 = x2d.reshape(batch, seq, D)
    yc3 = y_conv.reshape(batch, seq, kc)

    def cur(col):
        def index(t):
            tb = jnp.minimum(t, n_blocks - 1)
            return tb // nc, tb % nc, col
        return index

    def prev(t):
        tb = jnp.maximum(t - 1, 0)
        return tb // nc, tb % nc, 0

    cast_in, cast_out, cast_shapes = _cast_plan(
        cast_items, n_blocks, 1, lambda t: (jnp.minimum(t, n_blocks - 1), 0))
    x1, *cast = pl.pallas_call(
        functools.partial(_mlstm_out_kernel, dqk=dqk, dv=dv, n_cast=len(cast_items), nc=nc),
        grid=(n_blocks + 1,),
        in_specs=[
            pl.BlockSpec((bb, L, dq_all), cur(0)),
            pl.BlockSpec((bb, L, dq_all), cur(1)),
            pl.BlockSpec((bb, L, dv_all), cur(1)),
            pl.BlockSpec((bb, L, dv_all), cur(2)),
            pl.BlockSpec((bb, L, LANES), cur(0)),
            pl.BlockSpec((1, LANES), lambda t: (0, 0)),
            pl.BlockSpec((1, dv_all), lambda t: (0, 0)),
            pl.BlockSpec((bb, L, D), prev),
            pl.BlockSpec((bb, L, kc), prev),
            pl.BlockSpec((kc + dv_all, D), lambda t: (0, 0), pipeline_mode=pl.Buffered(1)),
        ] + cast_in,
        out_specs=[pl.BlockSpec((bb, L, D), prev)] + cast_out,
        out_shape=[jax.ShapeDtypeStruct((batch, seq, D), _F32)] + cast_shapes,
        scratch_shapes=[
            pltpu.VMEM((bb * N_HEADS, dqk, dv), _F32),
            pltpu.VMEM((SUBLANES, dqk), _F32),
            pltpu.VMEM((SUBLANES, LANES), _F32),
            pltpu.VMEM((bb, L, dv_all), _BF16),
        ],
        compiler_params=pltpu.CompilerParams(
            dimension_semantics=("arbitrary",), vmem_limit_bytes=VMEM_LIMIT),
        name="mlstm_out",
    )(qkvo3, qkvo3, qkvo3, qkvo3, gates3, gate_bias, gnorm, x3, yc3, w_out,
      *[a for a, _, _ in cast_items])
    return (x1.reshape(batch * seq, D), *cast)


def _ffn_kernel(x_ref, g_ref, wg_ref, wu_ref, cwg_ref, cwu_ref, bg_ref, bu_ref, wd_ref,
                o_ref, h_ref, halog_ref, halou_ref, *, tiles_per_seq, rows):
    m = pl.program_id(0)
    j = pl.program_id(1)
    tm = x_ref.shape[0]

    @pl.when(j == 0)
    def _():
        x = x_ref[...]
        h_ref[...] = _rmsnorm_rows(x, g_ref[...]).astype(_BF16)
        o_ref[...] = x

    first = m % tiles_per_seq == 0
    prev_g = _load_halo(halog_ref, j, first)
    prev_u = _load_halo(halou_ref, j, first)

    def up_proj(r):
        h = h_ref[r * rows:(r + 1) * rows, :]
        return _dot(h, wg_ref[...]), _dot(h, wu_ref[...])

    n_chunks = tm // rows
    cur = up_proj(0)
    for r in range(n_chunks):
        nxt = up_proj(r + 1) if r + 1 < n_chunks else None
        zg, zu = cur
        gate = _causal_conv3(zg, prev_g, cwg_ref[...]) + bg_ref[...]
        up = _causal_conv3(zu, prev_u, cwu_ref[...]) + bu_ref[...]
        act = (jax.nn.silu(gate) * up).astype(_BF16)
        o_ref[r * rows:(r + 1) * rows, :] += _dot(act, wd_ref[...])
        prev_g = zg[rows - SUBLANES:, :]
        prev_u = zu[rows - SUBLANES:, :]
        cur = nxt
    halog_ref[j] = prev_g
    halou_ref[j] = prev_u


def _ffn(x2d, g, w_up, cw, cb, w_down, *, seq, tm=1024, tf=512, rows=512):
    M, D = x2d.shape
    F = w_down.shape[0]
    nj = F // tf
    return pl.pallas_call(
        functools.partial(_ffn_kernel, tiles_per_seq=seq // tm, rows=rows),
        grid=(M // tm, nj),
        in_specs=[
            pl.BlockSpec((tm, D), lambda m, j: (m, 0)),
            pl.BlockSpec((1, D), lambda m, j: (0, 0)),
            pl.BlockSpec((D, tf), lambda m, j: (0, j)),
            pl.BlockSpec((D, tf), lambda m, j: (0, nj + j)),
            pl.BlockSpec((CONV_WIDTH, tf), lambda m, j: (0, j)),
            pl.BlockSpec((CONV_WIDTH, tf), lambda m, j: (0, nj + j)),
            pl.BlockSpec((1, tf), lambda m, j: (0, j)),
            pl.BlockSpec((1, tf), lambda m, j: (0, nj + j)),
            pl.BlockSpec((tf, D), lambda m, j: (j, 0)),
        ],
        out_specs=pl.BlockSpec((tm, D), lambda m, j: (m, 0)),
        out_shape=jax.ShapeDtypeStruct((M, D), _F32),
        scratch_shapes=[
            pltpu.VMEM((tm, D), _BF16),
            pltpu.VMEM((nj, SUBLANES, tf), _F32),
            pltpu.VMEM((nj, SUBLANES, tf), _F32),
        ],
        compiler_params=pltpu.CompilerParams(
            dimension_semantics=("arbitrary", "arbitrary"), vmem_limit_bytes=VMEM_LIMIT),
        name="ffn",
    )(x2d, g, w_up, w_up, cw, cw, cb, cb, w_down)


def _ple_kernel(x_ref, g_ref, p_ref, wg_ref, wp_ref, gf_ref, o_ref, *, final_norm, rows):
    def normed(r):
        return _rmsnorm_rows(x_ref[r * rows:(r + 1) * rows, :], g_ref[...]).astype(_BF16)

    n_chunks = x_ref.shape[0] // rows
    h = normed(0)
    for r in range(n_chunks):
        h_next = normed(r + 1) if r + 1 < n_chunks else None
        sl = slice(r * rows, (r + 1) * rows)
        gate = jax.nn.sigmoid(_dot(h, wg_ref[...]))
        x = x_ref[sl, :] + gate * _dot(p_ref[sl, :].astype(_BF16), wp_ref[...])
        if final_norm:
            x = _rmsnorm_rows(x, gf_ref[...])
        o_ref[sl, :] = x
        h = h_next


def _ple(x2d, g, p2d, w_gate, w_proj, g_final, *, final_norm, tm=512, rows=256):
    M, D = x2d.shape
    P = p2d.shape[1]
    return pl.pallas_call(
        functools.partial(_ple_kernel, final_norm=final_norm, rows=rows),
        grid=(M // tm,),
        in_specs=[
            pl.BlockSpec((tm, D), lambda m: (m, 0)),
            pl.BlockSpec((1, D), lambda m: (0, 0)),
            pl.BlockSpec((tm, P), lambda m: (m, 0)),
            pl.BlockSpec((D, D), lambda m: (0, 0)),
            pl.BlockSpec((P, D), lambda m: (0, 0)),
            pl.BlockSpec((1, D), lambda m: (0, 0)),
        ],
        out_specs=pl.BlockSpec((tm, D), lambda m: (m, 0)),
        out_shape=jax.ShapeDtypeStruct((M, D), _F32),
        compiler_params=pltpu.CompilerParams(
            dimension_semantics=("arbitrary",), vmem_limit_bytes=VMEM_LIMIT),
        name="ple",
    )(x2d, g, p2d, w_gate, w_proj, g_final)


def kernel(x, p, norm_mix_g, w_in, b_igate, b_fgate, short_conv_w, mh_norm_g, w_out,
           norm_ffn_g, w_up, ffn_conv_w, ffn_conv_b, w_down, norm_ple_g, w_ple_gate,
           w_ple_proj, final_norm_g):
    B, S, D = x.shape
    depth = w_in.shape[0]
    d_conv = short_conv_w.shape[-1]
    d_mlstm = mh_norm_g.shape[-1]
    dv = d_mlstm // N_HEADS
    d_qk = (w_in.shape[-1] - 3 * d_conv - 2 * d_mlstm - 2 * N_HEADS) // 2
    dqk = d_qk // N_HEADS
    M = B * S

    x2d = x.reshape(M, D)
    row = lambda a: a.reshape(1, -1).astype(_F32)
    colscale = jnp.concatenate([
        jnp.full((1, d_qk), dqk ** -0.5, _F32),
        jnp.ones((1, d_qk + 2 * d_mlstm), _F32)], axis=1)

    for i in range(depth):
        w = w_in[i]
        c0 = 3 * d_conv
        c1 = c0 + 2 * d_qk + 2 * d_mlstm
        w_t = jnp.swapaxes(w, 0, 1)
        w_gates = jnp.pad(w_t[c1:], ((0, LANES - 2 * N_HEADS), (0, 0))).astype(_BF16)
        gate_bias = jnp.pad(jnp.concatenate([b_igate[i], b_fgate[i]]).astype(_F32),
                            (0, LANES - 2 * N_HEADS)).reshape(1, LANES)

        g_mix = row(norm_mix_g[i])
        whole = lambda a: (a, 0, a.shape[0])
        y_conv, w_qkvo_t = _conv_group(x2d, g_mix, w_t, short_conv_w[i].astype(_F32),
                                       [(w_t, c0, c1 - c0)], seq=S)
        qkvo, gates, w_up_b, w_out_b = _qkvo(x2d, g_mix, w_qkvo_t, colscale, w_gates,
                                             [whole(w_up[i]), whole(w_out[i])])
        x2d, w_down_b, w_pg_b = _mlstm_out(
            qkvo, gates, gate_bias, row(mh_norm_g[i]), x2d, y_conv, w_out_b,
            [whole(w_down[i]), whole(w_ple_gate[i])], batch=B, seq=S, dqk=dqk, dv=dv)
        x2d = _ffn(x2d, row(norm_ffn_g[i]), w_up_b, ffn_conv_w[i].astype(_F32),
                   row(ffn_conv_b[i]), w_down_b, seq=S)
        x2d = _ple(x2d, row(norm_ple_g[i]), p[i].reshape(M, -1), w_pg_b,
                   w_ple_proj[i].astype(_BF16), row(final_norm_g),
                   final_norm=(i == depth - 1))
    return x2d.reshape(B, S, D)
```

```python
import functools

import jax
import jax.numpy as jnp
from jax import lax
from jax.experimental import pallas as pl
from jax.experimental.pallas import tpu as pltpu

N_HEADS = 4
CONV_WIDTH = 3
EPS = 1e-6
LANES = 128
SUBLANES = 8
MLSTM_CHUNK = 256
VMEM_LIMIT = 62 * 1024 * 1024

_BF16 = jnp.bfloat16
_F32 = jnp.float32


def _dot(a, b):
    return jnp.dot(a, b, preferred_element_type=_F32)


def _dot_nt(a, bt):
    return lax.dot_general(a, bt, (((1,), (1,)), ((), ())), preferred_element_type=_F32)


def _rmsnorm_rows(x, g):
    ms = jnp.mean(x * x, axis=-1, keepdims=True)
    return x * lax.rsqrt(ms + EPS) * g


def _causal_conv3(z, prev, cw):
    n = z.shape[0]
    ext = jnp.concatenate([prev, z], axis=0)
    z2 = ext[SUBLANES - 2:SUBLANES - 2 + n, :]
    z1 = ext[SUBLANES - 1:SUBLANES - 1 + n, :]
    return z2 * cw[0:1, :] + z1 * cw[1:2, :] + z * cw[2:3, :]


def _load_halo(halo_ref, j, first_tile):
    @pl.when(first_tile)
    def _():
        halo_ref[j] = jnp.zeros(halo_ref.shape[1:], _F32)

    return halo_ref[j]


def _cast_plan(items, n_steps, step_index):
    in_specs, out_specs, shapes = [], [], []
    for a, row0, n_rows in items:
        slab = n_rows // n_steps
        assert slab * n_steps == n_rows and slab % (2 * SUBLANES) == 0 and row0 % slab == 0
        cols = a.shape[1]
        in_specs.append(pl.BlockSpec(
            (slab, cols), lambda *idx, b0=row0 // slab: (b0 + step_index(*idx), 0)))
        out_specs.append(pl.BlockSpec((slab, cols), lambda *idx: (step_index(*idx), 0)))
        shapes.append(jax.ShapeDtypeStruct((n_rows, cols), _BF16))
    return in_specs, out_specs, shapes


def _cast_slabs(refs):
    half = len(refs) // 2
    for src, dst in zip(refs[:half], refs[half:]):
        dst[...] = src[...].astype(dst.dtype)


def _qkvo_kernel(x_ref, g_ref, w_ref, cs_ref, wgate_ref, *rest, n_cast):
    cast_in = rest[:n_cast]
    o_ref, gate_ref, h_ref, *cast_out = rest[n_cast:2 * n_cast + 3]
    (wb_ref,) = rest[2 * n_cast + 3:]
    m = pl.program_id(0)
    j = pl.program_id(1)

    _cast_slabs(list(cast_in) + list(cast_out))

    @pl.when(j == 0)
    def _():
        h = _rmsnorm_rows(x_ref[...], g_ref[...]).astype(_BF16)
        h_ref[...] = h
        gate_ref[...] = _dot_nt(h, wgate_ref[...])

    @pl.when(m == 0)
    def _():
        wb_ref[j] = w_ref[...].astype(_BF16)

    o_ref[...] = (_dot_nt(h_ref[...], wb_ref[j]) * cs_ref[...]).astype(o_ref.dtype)


def _qkvo(x2d, g, w_in_t, row0, colscale, wgate, cast_items, *, tm=1024, tn=1024):
    M, D = x2d.shape
    N = colscale.shape[1]
    assert row0 % tn == 0 and N % tn == 0
    nj = N // tn
    cast_in, cast_out, cast_shapes = _cast_plan(
        cast_items, (M // tm) * nj, lambda m, j: m * nj + j)
    wtile = lambda m, j: (row0 // tn + jnp.where(m == 0, j, nj - 1), 0)
    return pl.pallas_call(
        functools.partial(_qkvo_kernel, n_cast=len(cast_items)),
        grid=(M // tm, nj),
        in_specs=[
            pl.BlockSpec((tm, D), lambda m, j: (m, 0)),
            pl.BlockSpec((1, D), lambda m, j: (0, 0)),
            pl.BlockSpec((tn, D), wtile, pipeline_mode=pl.Buffered(1)),
            pl.BlockSpec((1, tn), lambda m, j: (0, j)),
            pl.BlockSpec((LANES, D), lambda m, j: (0, 0)),
        ] + cast_in,
        out_specs=[
            pl.BlockSpec((tm, tn), lambda m, j: (m, j)),
            pl.BlockSpec((tm, LANES), lambda m, j: (m, 0)),
            pl.BlockSpec((tm, D), lambda m, j: (m, 0)),
        ] + cast_out,
        out_shape=[
            jax.ShapeDtypeStruct((M, N), _BF16),
            jax.ShapeDtypeStruct((M, LANES), _F32),
            jax.ShapeDtypeStruct((M, D), _BF16),
        ] + cast_shapes,
        scratch_shapes=[pltpu.VMEM((nj, tn, D), _BF16)],
        compiler_params=pltpu.CompilerParams(
            dimension_semantics=("arbitrary", "arbitrary"), vmem_limit_bytes=VMEM_LIMIT),
        name="qkvo",
    )(x2d, g, w_in_t, colscale, wgate, *[a for a, _, _ in cast_items])


def _conv_group_kernel(h_ref, w_ref, cw_ref, *rest, tiles_per_seq, rows, n_cast):
    cast_in, (o_ref, *cast_out) = rest[:n_cast], rest[n_cast:2 * n_cast + 1]
    (halo_ref,) = rest[2 * n_cast + 1:]
    m = pl.program_id(0)
    j = pl.program_id(1)
    tm, tc = o_ref.shape
    n_ch = w_ref.shape[0] // 3

    _cast_slabs(list(cast_in) + list(cast_out))

    def wtile(part):
        return w_ref[pl.ds(pl.multiple_of(part * n_ch + j * tc, tc), tc), :]

    def proj(r):
        h = h_ref[r * rows:(r + 1) * rows, :]
        return _dot_nt(h, wtile(0)), _dot_nt(h, wtile(1)), _dot_nt(h, wtile(2))

    prev = _load_halo(halo_ref, j, m % tiles_per_seq == 0)
    n_chunks = tm // rows
    cur = proj(0)
    for r in range(n_chunks):
        nxt = proj(r + 1) if r + 1 < n_chunks else None
        gb, gc, u = cur
        z = gc * u
        y = _causal_conv3(z, prev, cw_ref[...])
        o_ref[r * rows:(r + 1) * rows, :] = (gb * y).astype(o_ref.dtype)
        prev = z[rows - SUBLANES:, :]
        cur = nxt
    halo_ref[j] = prev


def _conv_group(h2d, w_t, cw, cast_items, *, seq, tm=1024, tc=512, rows=256):
    M, D = h2d.shape
    C = cw.shape[1]
    assert w_t.shape == (3 * C, D) and C % tc == 0
    nj = C // tc
    cast_in, cast_out, cast_shapes = _cast_plan(
        cast_items, (M // tm) * nj, lambda m, j: m * nj + j)
    return pl.pallas_call(
        functools.partial(_conv_group_kernel, tiles_per_seq=seq // tm, rows=rows,
                          n_cast=len(cast_items)),
        grid=(M // tm, nj),
        in_specs=[
            pl.BlockSpec((tm, D), lambda m, j: (m, 0)),
            pl.BlockSpec((3 * C, D), lambda m, j: (0, 0), pipeline_mode=pl.Buffered(1)),
            pl.BlockSpec((CONV_WIDTH, tc), lambda m, j: (0, j)),
        ] + cast_in,
        out_specs=[pl.BlockSpec((tm, tc), lambda m, j: (m, j))] + cast_out,
        out_shape=[jax.ShapeDtypeStruct((M, C), _BF16)] + cast_shapes,
        scratch_shapes=[pltpu.VMEM((nj, SUBLANES, tc), _F32)],
        compiler_params=pltpu.CompilerParams(
            dimension_semantics=("arbitrary", "arbitrary"), vmem_limit_bytes=VMEM_LIMIT),
        name="conv_group",
    )(h2d, w_t, cw, *[a for a, _, _ in cast_items])


def _exact_cumsum_rows(tril, a):
    hi = a.astype(_BF16)
    r1 = a - hi.astype(_F32)
    mid = r1.astype(_BF16)
    lo = (r1 - mid.astype(_F32)).astype(_BF16)
    return _dot(tril, hi) + _dot(tril, mid) + _dot(tril, lo)


def _mlstm_out_kernel(q_ref, k_ref, v_ref, o_ref, gate_ref, bias_ref, gn_ref,
                      x_ref, yc_ref, wo_ref, *rest, dqk, dv, n_cast, nc):
    cast_in, (x1_ref, *cast_out) = rest[:n_cast], rest[n_cast:2 * n_cast + 1]
    c_ref, n_ref, m_ref, ym_ref = rest[2 * n_cast + 1:]
    t = pl.program_id(0)
    bb, L = q_ref.shape[0], q_ref.shape[1]
    kc = yc_ref.shape[2]

    _cast_slabs(list(cast_in) + list(cast_out))

    @pl.when(t % nc == 0)
    def _():
        c_ref[...] = jnp.zeros(c_ref.shape, _F32)
        n_ref[...] = jnp.zeros(n_ref.shape, _F32)
        m_ref[...] = jnp.zeros(m_ref.shape, _F32)

    @pl.when(t == 0)
    def _():
        ym_ref[...] = jnp.zeros(ym_ref.shape, ym_ref.dtype)

    def out_proj(bi):
        acc = _dot(yc_ref[bi], wo_ref[0:kc, :]) + _dot(ym_ref[bi], wo_ref[kc:, :])
        x1_ref[bi] = x_ref[bi] + acc

    row = lax.broadcasted_iota(jnp.int32, (L, L), 0)
    col = lax.broadcasted_iota(jnp.int32, (L, L), 1)
    causal = col <= row
    tril = causal.astype(_BF16)

    chains = [(bi, h) for bi in range(bb) for h in range(N_HEADS)]
    pre = {}
    for bi in range(bb):
        gates = gate_ref[bi] + bias_ref[...]
        logf = jnp.minimum(gates, 0.0) - jnp.log1p(jnp.exp(-jnp.abs(gates)))
        bcum = _exact_cumsum_rows(tril, logf)
        pre[bi] = (gates, bcum, gates.T, bcum.T)

    stab = {}
    for bi, h in chains:
        gates, bcum, gates_t, bcum_t = pre[bi]
        st = bi * N_HEADS + h
        b_col = bcum[:, N_HEADS + h:N_HEADS + h + 1]
        r_row = gates_t[h:h + 1, :] - bcum_t[N_HEADS + h:N_HEADS + h + 1, :]
        dlog = jnp.where(causal, b_col + r_row, -jnp.inf)
        inter = b_col + m_ref[st:st + 1, 0:1]
        m_t = jnp.maximum(inter, jnp.max(dlog, axis=-1, keepdims=True))
        stab[bi, h] = (b_col, dlog, inter, m_t)

    for bi in range(0, bb, 2):
        out_proj(bi)

    out = {}
    for bi, h in chains:
        st = bi * N_HEADS + h
        b_col, dlog, inter, m_t = stab[bi, h]
        dw = jnp.exp(dlog - m_t)
        a_inter = jnp.exp(inter - m_t)
        q = q_ref[bi, :, h * dqk:(h + 1) * dqk]
        k = k_ref[bi, :, h * dqk:(h + 1) * dqk]
        v = v_ref[bi, :, h * dv:(h + 1) * dv]
        s = _dot_nt(q, k) * dw
        num = a_inter * _dot(q, c_ref[st].astype(_BF16)) + _dot(s.astype(_BF16), v)
        qn = jnp.sum(q.astype(_F32) * n_ref[st:st + 1, :], axis=-1, keepdims=True)
        den = a_inter * qn + jnp.sum(s, axis=-1, keepdims=True)
        den = jnp.maximum(jnp.abs(den), jnp.exp(-m_t))
        out[bi, h] = (num / den, a_inter)

    for bi in range(1, bb, 2):
        out_proj(bi)

    for bi, h in chains:
        hm, _ = out[bi, h]
        hm = hm * lax.rsqrt(jnp.mean(hm * hm, axis=-1, keepdims=True) + EPS)
        hm = hm * gn_ref[:, h * dv:(h + 1) * dv]
        og = jax.nn.sigmoid(o_ref[bi, :, h * dv:(h + 1) * dv].astype(_F32))
        ym_ref[bi, :, h * dv:(h + 1) * dv] = (og * hm).astype(ym_ref.dtype)

    for bi, h in chains:
        st = bi * N_HEADS + h
        gates = pre[bi][0]
        b_col, _, _, m_t = stab[bi, h]
        a_inter = out[bi, h][1]
        k = k_ref[bi, :, h * dqk:(h + 1) * dqk]
        v = v_ref[bi, :, h * dv:(h + 1) * dv]
        b_tot = b_col[L - 1:L, :]
        m_new = m_t[L - 1:L, :]
        a_prev = a_inter[L - 1:L, :]
        w_col = jnp.exp(b_tot - b_col + gates[:, h:h + 1] - m_new)
        kw = k.astype(_F32) * w_col
        c_ref[st] = a_prev * c_ref[st] + _dot(kw.T.astype(_BF16), v)
        n_ref[st:st + 1, :] = a_prev * n_ref[st:st + 1, :] + jnp.sum(kw, axis=0, keepdims=True)
        m_ref[st:st + 1, :] = jnp.broadcast_to(m_new, (1, LANES))


def _mlstm_out(qkvo, gates, gate_bias, gnorm, x2d, y_conv, w_out, cast_items, *,
               batch, seq, dqk, dv, bb=2):
    L = MLSTM_CHUNK
    nc = seq // L
    D = x2d.shape[1]
    dq_all = N_HEADS * dqk
    dv_all = N_HEADS * dv
    kc = y_conv.shape[1]
    assert dv_all == 2 * dq_all and batch % bb == 0 and bb * N_HEADS <= SUBLANES
    assert w_out.shape == (kc + dv_all, D)
    n_blocks = (batch // bb) * nc
    qkvo3 = qkvo.reshape(batch, seq, -1)
    gates3 = gates.reshape(batch, seq, LANES)
    x3 = x2d.reshape(batch, seq, D)
    yc3 = y_conv.reshape(batch, seq, kc)

    def cur(col):
        def index(t):
            tb = jnp.minimum(t, n_blocks - 1)
            return tb // nc, tb % nc, col
        return index

    def prev(t):
        tb = jnp.maximum(t - 1, 0)
        return tb // nc, tb % nc, 0

    cast_in, cast_out, cast_shapes = _cast_plan(
        cast_items, n_blocks, lambda t: jnp.minimum(t, n_blocks - 1))
    x1, *cast = pl.pallas_call(
        functools.partial(_mlstm_out_kernel, dqk=dqk, dv=dv, n_cast=len(cast_items), nc=nc),
        grid=(n_blocks + 1,),
        in_specs=[
            pl.BlockSpec((bb, L, dq_all), cur(0)),
            pl.BlockSpec((bb, L, dq_all), cur(1)),
            pl.BlockSpec((bb, L, dv_all), cur(1)),
            pl.BlockSpec((bb, L, dv_all), cur(2)),
            pl.BlockSpec((bb, L, LANES), cur(0)),
            pl.BlockSpec((1, LANES), lambda t: (0, 0)),
            pl.BlockSpec((1, dv_all), lambda t: (0, 0)),
            pl.BlockSpec((bb, L, D), prev),
            pl.BlockSpec((bb, L, kc), prev),
            pl.BlockSpec((kc + dv_all, D), lambda t: (0, 0), pipeline_mode=pl.Buffered(1)),
        ] + cast_in,
        out_specs=[pl.BlockSpec((bb, L, D), prev)] + cast_out,
        out_shape=[jax.ShapeDtypeStruct((batch, seq, D), _F32)] + cast_shapes,
        scratch_shapes=[
            pltpu.VMEM((bb * N_HEADS, dqk, dv), _F32),
            pltpu.VMEM((SUBLANES, dqk), _F32),
            pltpu.VMEM((SUBLANES, LANES), _F32),
            pltpu.VMEM((bb, L, dv_all), _BF16),
        ],
        compiler_params=pltpu.CompilerParams(
            dimension_semantics=("arbitrary",), vmem_limit_bytes=VMEM_LIMIT),
        name="mlstm_out",
    )(qkvo3, qkvo3, qkvo3, qkvo3, gates3, gate_bias, gnorm, x3, yc3, w_out,
      *[a for a, _, _ in cast_items])
    return (x1.reshape(batch * seq, D), *cast)


def _ffn_kernel(x_ref, g_ref, wg_ref, wu_ref, cwg_ref, cwu_ref, bg_ref, bu_ref, wd_ref,
                o_ref, h_ref, halog_ref, halou_ref, *, tiles_per_seq, rows):
    m = pl.program_id(0)
    j = pl.program_id(1)
    tm = x_ref.shape[0]

    @pl.when(j == 0)
    def _():
        x = x_ref[...]
        h_ref[...] = _rmsnorm_rows(x, g_ref[...]).astype(_BF16)
        o_ref[...] = x

    first = m % tiles_per_seq == 0
    prev_g = _load_halo(halog_ref, j, first)
    prev_u = _load_halo(halou_ref, j, first)

    def up_proj(r):
        h = h_ref[r * rows:(r + 1) * rows, :]
        return _dot(h, wg_ref[...]), _dot(h, wu_ref[...])

    n_chunks = tm // rows
    cur = up_proj(0)
    for r in range(n_chunks):
        nxt = up_proj(r + 1) if r + 1 < n_chunks else None
        zg, zu = cur
        gate = _causal_conv3(zg, prev_g, cwg_ref[...]) + bg_ref[...]
        up = _causal_conv3(zu, prev_u, cwu_ref[...]) + bu_ref[...]
        act = (jax.nn.silu(gate) * up).astype(_BF16)
        o_ref[r * rows:(r + 1) * rows, :] += _dot(act, wd_ref[...])
        prev_g = zg[rows - SUBLANES:, :]
        prev_u = zu[rows - SUBLANES:, :]
        cur = nxt
    halog_ref[j] = prev_g
    halou_ref[j] = prev_u


def _ffn(x2d, g, w_up, cw, cb, w_down, *, seq, tm=1024, tf=512, rows=512):
    M, D = x2d.shape
    F = w_down.shape[0]
    nj = F // tf
    return pl.pallas_call(
        functools.partial(_ffn_kernel, tiles_per_seq=seq // tm, rows=rows),
        grid=(M // tm, nj),
        in_specs=[
            pl.BlockSpec((tm, D), lambda m, j: (m, 0)),
            pl.BlockSpec((1, D), lambda m, j: (0, 0)),
            pl.BlockSpec((D, tf), lambda m, j: (0, j)),
            pl.BlockSpec((D, tf), lambda m, j: (0, nj + j)),
            pl.BlockSpec((CONV_WIDTH, tf), lambda m, j: (0, j)),
            pl.BlockSpec((CONV_WIDTH, tf), lambda m, j: (0, nj + j)),
            pl.BlockSpec((1, tf), lambda m, j: (0, j)),
            pl.BlockSpec((1, tf), lambda m, j: (0, nj + j)),
            pl.BlockSpec((tf, D), lambda m, j: (j, 0)),
        ],
        out_specs=pl.BlockSpec((tm, D), lambda m, j: (m, 0)),
        out_shape=jax.ShapeDtypeStruct((M, D), _F32),
        scratch_shapes=[
            pltpu.VMEM((tm, D), _BF16),
            pltpu.VMEM((nj, SUBLANES, tf), _F32),
            pltpu.VMEM((nj, SUBLANES, tf), _F32),
        ],
        compiler_params=pltpu.CompilerParams(
            dimension_semantics=("arbitrary", "arbitrary"), vmem_limit_bytes=VMEM_LIMIT),
        name="ffn",
    )(x2d, g, w_up, w_up, cw, cw, cb, cb, w_down)


def _ple_kernel(x_ref, g_ref, p_ref, wg_ref, wp_ref, gf_ref, o_ref, *, final_norm, rows):
    def normed(r):
        return _rmsnorm_rows(x_ref[r * rows:(r + 1) * rows, :], g_ref[...]).astype(_BF16)

    n_chunks = x_ref.shape[0] // rows
    h = normed(0)
    for r in range(n_chunks):
        h_next = normed(r + 1) if r + 1 < n_chunks else None
        sl = slice(r * rows, (r + 1) * rows)
        gate = jax.nn.sigmoid(_dot(h, wg_ref[...]))
        x = x_ref[sl, :] + gate * _dot(p_ref[sl, :].astype(_BF16), wp_ref[...])
        if final_norm:
            x = _rmsnorm_rows(x, gf_ref[...])
        o_ref[sl, :] = x
        h = h_next


def _ple(x2d, g, p2d, w_gate, w_proj, g_final, *, final_norm, tm=512, rows=256):
    M, D = x2d.shape
    P = p2d.shape[1]
    return pl.pallas_call(
        functools.partial(_ple_kernel, final_norm=final_norm, rows=rows),
        grid=(M // tm,),
        in_specs=[
            pl.BlockSpec((tm, D), lambda m: (m, 0)),
            pl.BlockSpec((1, D), lambda m: (0, 0)),
            pl.BlockSpec((tm, P), lambda m: (m, 0)),
            pl.BlockSpec((D, D), lambda m: (0, 0)),
            pl.BlockSpec((P, D), lambda m: (0, 0)),
            pl.BlockSpec((1, D), lambda m: (0, 0)),
        ],
        out_specs=pl.BlockSpec((tm, D), lambda m: (m, 0)),
        out_shape=jax.ShapeDtypeStruct((M, D), _F32),
        compiler_params=pltpu.CompilerParams(
            dimension_semantics=("arbitrary",), vmem_limit_bytes=VMEM_LIMIT),
        name="ple",
    )(x2d, g, p2d, w_gate, w_proj, g_final)


def kernel(x, p, norm_mix_g, w_in, b_igate, b_fgate, short_conv_w, mh_norm_g, w_out,
           norm_ffn_g, w_up, ffn_conv_w, ffn_conv_b, w_down, norm_ple_g, w_ple_gate,
           w_ple_proj, final_norm_g):
    B, S, D = x.shape
    depth = w_in.shape[0]
    d_conv = short_conv_w.shape[-1]
    d_mlstm = mh_norm_g.shape[-1]
    dv = d_mlstm // N_HEADS
    d_qk = (w_in.shape[-1] - 3 * d_conv - 2 * d_mlstm - 2 * N_HEADS) // 2
    dqk = d_qk // N_HEADS
    M = B * S

    x2d = x.reshape(M, D)
    row = lambda a: a.reshape(1, -1).astype(_F32)
    whole = lambda a: (a, 0, a.shape[0])
    colscale = jnp.concatenate([
        jnp.full((1, d_qk), dqk ** -0.5, _F32),
        jnp.ones((1, d_qk + 2 * d_mlstm), _F32)], axis=1)

    for i in range(depth):
        c0 = 3 * d_conv
        c1 = c0 + 2 * d_qk + 2 * d_mlstm
        w_t = jnp.swapaxes(w_in[i], 0, 1)
        w_gates = jnp.pad(w_t[c1:], ((0, LANES - 2 * N_HEADS), (0, 0))).astype(_BF16)
        gate_bias = jnp.pad(jnp.concatenate([b_igate[i], b_fgate[i]]).astype(_F32),
                            (0, LANES - 2 * N_HEADS)).reshape(1, LANES)

        qkvo, gates, h, w_conv_t = _qkvo(x2d, row(norm_mix_g[i]), w_t, c0, colscale, w_gates,
                                         [(w_t, 0, c0)])
        y_conv, w_up_b, w_out_b = _conv_group(h, w_conv_t, short_conv_w[i].astype(_F32),
                                              [whole(w_up[i]), whole(w_out[i])], seq=S)
        x2d, w_down_b, w_pg_b = _mlstm_out(
            qkvo, gates, gate_bias, row(mh_norm_g[i]), x2d, y_conv, w_out_b,
            [whole(w_down[i]), whole(w_ple_gate[i])], batch=B, seq=S, dqk=dqk, dv=dv)
        x2d = _ffn(x2d, row(norm_ffn_g[i]), w_up_b, ffn_conv_w[i].astype(_F32),
                   row(ffn_conv_b[i]), w_down_b, seq=S)
        x2d = _ple(x2d, row(norm_ple_g[i]), p[i].reshape(M, -1), w_pg_b,
                   w_ple_proj[i].astype(_BF16), row(final_norm_g),
                   final_norm=(i == depth - 1))
    return x2d.reshape(B, S, D)
```

```python
import functools

import jax
import jax.numpy as jnp
from jax import lax
from jax.experimental import pallas as pl
from jax.experimental.pallas import tpu as pltpu

N_HEADS = 4
CONV_WIDTH = 3
EPS = 1e-6
LANES = 128
SUBLANES = 8
MLSTM_CHUNK = 256
V7X_VMEM_BYTES = 64 * 1024 * 1024
VMEM_LIMIT = V7X_VMEM_BYTES - 2 * 1024 * 1024

_BF16 = jnp.bfloat16
_F32 = jnp.float32


def _dot(a, b):
    return jnp.dot(a, b, preferred_element_type=_F32)


def _dot_nt(a, bt):
    return lax.dot_general(a, bt, (((1,), (1,)), ((), ())), preferred_element_type=_F32)


def _rmsnorm_rows(x, g):
    ms = jnp.mean(x * x, axis=-1, keepdims=True)
    return x * lax.rsqrt(ms + EPS) * g


def _causal_conv3(z, prev, cw):
    n = z.shape[0]
    ext = jnp.concatenate([prev, z], axis=0)
    z2 = ext[SUBLANES - 2:SUBLANES - 2 + n, :]
    z1 = ext[SUBLANES - 1:SUBLANES - 1 + n, :]
    return z2 * cw[0:1, :] + z1 * cw[1:2, :] + z * cw[2:3, :]


def _load_halo(halo_ref, j, first_tile):
    @pl.when(first_tile)
    def _():
        halo_ref[j] = jnp.zeros(halo_ref.shape[1:], _F32)

    return halo_ref[j]


def _cast_plan(items, n_steps, step_index):
    in_specs, out_specs, shapes = [], [], []
    for a, row0, n_rows in items:
        slab = n_rows // n_steps
        assert slab * n_steps == n_rows and slab % (2 * SUBLANES) == 0 and row0 % slab == 0
        cols = a.shape[1]
        in_specs.append(pl.BlockSpec(
            (slab, cols), lambda *idx, b0=row0 // slab: (b0 + step_index(*idx), 0)))
        out_specs.append(pl.BlockSpec((slab, cols), lambda *idx: (step_index(*idx), 0)))
        shapes.append(jax.ShapeDtypeStruct((n_rows, cols), _BF16))
    return in_specs, out_specs, shapes


def _cast_slabs(refs):
    half = len(refs) // 2
    for src, dst in zip(refs[:half], refs[half:]):
        dst[...] = src[...].astype(dst.dtype)


def _qkvo_kernel(x_ref, g_ref, w_ref, cs_ref, wgate_ref, *rest, n_cast):
    cast_in = rest[:n_cast]
    o_ref, gate_ref, h_ref, *cast_out = rest[n_cast:2 * n_cast + 3]
    (wb_ref,) = rest[2 * n_cast + 3:]
    m = pl.program_id(0)
    j = pl.program_id(1)

    _cast_slabs(list(cast_in) + list(cast_out))

    @pl.when(j == 0)
    def _():
        h = _rmsnorm_rows(x_ref[...], g_ref[...]).astype(_BF16)
        h_ref[...] = h
        gate_ref[...] = _dot_nt(h, wgate_ref[...])

    @pl.when(m == 0)
    def _():
        wb_ref[j] = w_ref[...].astype(_BF16)

    o_ref[...] = (_dot_nt(h_ref[...], wb_ref[j]) * cs_ref[...]).astype(o_ref.dtype)


def _qkvo(x2d, g, w_in_t, row0, colscale, wgate, cast_items, *, tm=1024, tn=1024):
    M, D = x2d.shape
    N = colscale.shape[1]
    assert row0 % tn == 0 and N % tn == 0
    nj = N // tn
    cast_in, cast_out, cast_shapes = _cast_plan(
        cast_items, (M // tm) * nj, lambda m, j: m * nj + j)
    wtile = lambda m, j: (row0 // tn + jnp.where(m == 0, j, nj - 1), 0)
    return pl.pallas_call(
        functools.partial(_qkvo_kernel, n_cast=len(cast_items)),
        grid=(M // tm, nj),
        in_specs=[
            pl.BlockSpec((tm, D), lambda m, j: (m, 0)),
            pl.BlockSpec((1, D), lambda m, j: (0, 0)),
            pl.BlockSpec((tn, D), wtile, pipeline_mode=pl.Buffered(1)),
            pl.BlockSpec((1, tn), lambda m, j: (0, j)),
            pl.BlockSpec((LANES, D), lambda m, j: (0, 0)),
        ] + cast_in,
        out_specs=[
            pl.BlockSpec((tm, tn), lambda m, j: (m, j)),
            pl.BlockSpec((tm, LANES), lambda m, j: (m, 0)),
            pl.BlockSpec((tm, D), lambda m, j: (m, 0)),
        ] + cast_out,
        out_shape=[
            jax.ShapeDtypeStruct((M, N), _BF16),
            jax.ShapeDtypeStruct((M, LANES), _F32),
            jax.ShapeDtypeStruct((M, D), _BF16),
        ] + cast_shapes,
        scratch_shapes=[pltpu.VMEM((nj, tn, D), _BF16)],
        compiler_params=pltpu.CompilerParams(
            dimension_semantics=("arbitrary", "arbitrary"), vmem_limit_bytes=VMEM_LIMIT),
        name="qkvo",
    )(x2d, g, w_in_t, colscale, wgate, *[a for a, _, _ in cast_items])


def _conv_group_kernel(h_ref, w_ref, cw_ref, *rest, tiles_per_seq, rows, n_cast):
    cast_in, (o_ref, *cast_out) = rest[:n_cast], rest[n_cast:2 * n_cast + 1]
    (halo_ref,) = rest[2 * n_cast + 1:]
    m = pl.program_id(0)
    j = pl.program_id(1)
    tm, tc = o_ref.shape
    n_ch = w_ref.shape[0] // 3

    _cast_slabs(list(cast_in) + list(cast_out))

    def wtile(part):
        return w_ref[pl.ds(pl.multiple_of(part * n_ch + j * tc, tc), tc), :]

    def proj(r):
        h = h_ref[r * rows:(r + 1) * rows, :]
        return _dot_nt(h, wtile(0)), _dot_nt(h, wtile(1)), _dot_nt(h, wtile(2))

    prev = _load_halo(halo_ref, j, m % tiles_per_seq == 0)
    n_chunks = tm // rows
    cur = proj(0)
    for r in range(n_chunks):
        nxt = proj(r + 1) if r + 1 < n_chunks else None
        gb, gc, u = cur
        z = gc * u
        y = _causal_conv3(z, prev, cw_ref[...])
        o_ref[r * rows:(r + 1) * rows, :] = (gb * y).astype(o_ref.dtype)
        prev = z[rows - SUBLANES:, :]
        cur = nxt
    halo_ref[j] = prev


def _conv_group(h2d, w_t, cw, cast_items, *, seq, tm=1024, tc=512, rows=256):
    M, D = h2d.shape
    C = cw.shape[1]
    assert w_t.shape == (3 * C, D) and C % tc == 0
    nj = C // tc
    cast_in, cast_out, cast_shapes = _cast_plan(
        cast_items, (M // tm) * nj, lambda m, j: m * nj + j)
    return pl.pallas_call(
        functools.partial(_conv_group_kernel, tiles_per_seq=seq // tm, rows=rows,
                          n_cast=len(cast_items)),
        grid=(M // tm, nj),
        in_specs=[
            pl.BlockSpec((tm, D), lambda m, j: (m, 0)),
            pl.BlockSpec((3 * C, D), lambda m, j: (0, 0), pipeline_mode=pl.Buffered(1)),
            pl.BlockSpec((CONV_WIDTH, tc), lambda m, j: (0, j)),
        ] + cast_in,
        out_specs=[pl.BlockSpec((tm, tc), lambda m, j: (m, j))] + cast_out,
        out_shape=[jax.ShapeDtypeStruct((M, C), _BF16)] + cast_shapes,
        scratch_shapes=[pltpu.VMEM((nj, SUBLANES, tc), _F32)],
        compiler_params=pltpu.CompilerParams(
            dimension_semantics=("arbitrary", "arbitrary"), vmem_limit_bytes=VMEM_LIMIT),
        name="conv_group",
    )(h2d, w_t, cw, *[a for a, _, _ in cast_items])


def _exact_cumsum_rows(tril, a):
    hi = a.astype(_BF16)
    r1 = a - hi.astype(_F32)
    mid = r1.astype(_BF16)
    lo = (r1 - mid.astype(_F32)).astype(_BF16)
    return _dot(tril, hi) + _dot(tril, mid) + _dot(tril, lo)


def _mlstm_out_kernel(q_ref, k_ref, v_ref, o_ref, gate_ref, bias_ref, gn_ref,
                      x_ref, yc_ref, wo_ref, *rest, dqk, dv, n_cast, nc):
    cast_in, (x1_ref, *cast_out) = rest[:n_cast], rest[n_cast:2 * n_cast + 1]
    c_ref, n_ref, m_ref, ym_ref = rest[2 * n_cast + 1:]
    t = pl.program_id(0)
    bb, L = q_ref.shape[0], q_ref.shape[1]
    kc = yc_ref.shape[2]

    _cast_slabs(list(cast_in) + list(cast_out))

    @pl.when(t % nc == 0)
    def _():
        c_ref[...] = jnp.zeros(c_ref.shape, _F32)
        n_ref[...] = jnp.zeros(n_ref.shape, _F32)
        m_ref[...] = jnp.zeros(m_ref.shape, _F32)

    @pl.when(t == 0)
    def _():
        ym_ref[...] = jnp.zeros(ym_ref.shape, ym_ref.dtype)

    def out_proj(bi):
        acc = _dot(yc_ref[bi], wo_ref[0:kc, :]) + _dot(ym_ref[bi], wo_ref[kc:, :])
        x1_ref[bi] = x_ref[bi] + acc

    row = lax.broadcasted_iota(jnp.int32, (L, L), 0)
    col = lax.broadcasted_iota(jnp.int32, (L, L), 1)
    causal = col <= row
    tril = causal.astype(_BF16)

    chains = [(bi, h) for bi in range(bb) for h in range(N_HEADS)]
    pre = {}
    for bi in range(bb):
        gates = gate_ref[bi] + bias_ref[...]
        logf = jnp.minimum(gates, 0.0) - jnp.log1p(jnp.exp(-jnp.abs(gates)))
        bcum = _exact_cumsum_rows(tril, logf)
        pre[bi] = (gates, bcum, gates.T, bcum.T)

    stab = {}
    for bi, h in chains:
        gates, bcum, gates_t, bcum_t = pre[bi]
        st = bi * N_HEADS + h
        b_col = bcum[:, N_HEADS + h:N_HEADS + h + 1]
        r_row = gates_t[h:h + 1, :] - bcum_t[N_HEADS + h:N_HEADS + h + 1, :]
        dlog = jnp.where(causal, b_col + r_row, -jnp.inf)
        inter = b_col + m_ref[st:st + 1, 0:1]
        m_t = jnp.maximum(inter, jnp.max(dlog, axis=-1, keepdims=True))
        stab[bi, h] = (b_col, dlog, inter, m_t)

    for bi in range(0, bb, 2):
        out_proj(bi)

    out = {}
    for bi, h in chains:
        st = bi * N_HEADS + h
        b_col, dlog, inter, m_t = stab[bi, h]
        dw = jnp.exp(dlog - m_t)
        a_inter = jnp.exp(inter - m_t)
        q = q_ref[bi, :, h * dqk:(h + 1) * dqk]
        k = k_ref[bi, :, h * dqk:(h + 1) * dqk]
        v = v_ref[bi, :, h * dv:(h + 1) * dv]
        s = _dot_nt(q, k) * dw
        num = a_inter * _dot(q, c_ref[st].astype(_BF16)) + _dot(s.astype(_BF16), v)
        qn = jnp.sum(q.astype(_F32) * n_ref[st:st + 1, :], axis=-1, keepdims=True)
        den = a_inter * qn + jnp.sum(s, axis=-1, keepdims=True)
        den = jnp.maximum(jnp.abs(den), jnp.exp(-m_t))
        out[bi, h] = (num / den, a_inter)

    for bi in range(1, bb, 2):
        out_proj(bi)

    for bi, h in chains:
        hm, _ = out[bi, h]
        hm = hm * lax.rsqrt(jnp.mean(hm * hm, axis=-1, keepdims=True) + EPS)
        hm = hm * gn_ref[:, h * dv:(h + 1) * dv]
        og = jax.nn.sigmoid(o_ref[bi, :, h * dv:(h + 1) * dv].astype(_F32))
        ym_ref[bi, :, h * dv:(h + 1) * dv] = (og * hm).astype(ym_ref.dtype)

    for bi, h in chains:
        st = bi * N_HEADS + h
        gates = pre[bi][0]
        b_col, _, _, m_t = stab[bi, h]
        a_inter = out[bi, h][1]
        k = k_ref[bi, :, h * dqk:(h + 1) * dqk]
        v = v_ref[bi, :, h * dv:(h + 1) * dv]
        b_tot = b_col[L - 1:L, :]
        m_new = m_t[L - 1:L, :]
        a_prev = a_inter[L - 1:L, :]
        w_col = jnp.exp(b_tot - b_col + gates[:, h:h + 1] - m_new)
        kw = k.astype(_F32) * w_col
        c_ref[st] = a_prev * c_ref[st] + _dot(kw.T.astype(_BF16), v)
        n_ref[st:st + 1, :] = a_prev * n_ref[st:st + 1, :] + jnp.sum(kw, axis=0, keepdims=True)
        m_ref[st:st + 1, :] = jnp.broadcast_to(m_new, (1, LANES))


def _mlstm_out(qkvo, gates, gate_bias, gnorm, x2d, y_conv, w_out, cast_items, *,
               batch, seq, dqk, dv, bb=2):
    L = MLSTM_CHUNK
    nc = seq // L
    D = x2d.shape[1]
    dq_all = N_HEADS * dqk
    dv_all = N_HEADS * dv
    kc = y_conv.shape[1]
    assert dv_all == 2 * dq_all and batch % bb == 0 and bb * N_HEADS <= SUBLANES
    assert w_out.shape == (kc + dv_all, D)
    n_blocks = (batch // bb) * nc
    qkvo3 = qkvo.reshape(batch, seq, -1)
    gates3 = gates.reshape(batch, seq, LANES)
    x3 = x2d.reshape(batch, seq, D)
    yc3 = y_conv.reshape(batch, seq, kc)

    def cur(col):
        def index(t):
            tb = jnp.minimum(t, n_blocks - 1)
            return tb // nc, tb % nc, col
        return index

    def prev(t):
        tb = jnp.maximum(t - 1, 0)
        return tb // nc, tb % nc, 0

    cast_in, cast_out, cast_shapes = _cast_plan(
        cast_items, n_blocks, lambda t: jnp.minimum(t, n_blocks - 1))
    x1, *cast = pl.pallas_call(
        functools.partial(_mlstm_out_kernel, dqk=dqk, dv=dv, n_cast=len(cast_items), nc=nc),
        grid=(n_blocks + 1,),
        in_specs=[
            pl.BlockSpec((bb, L, dq_all), cur(0)),
            pl.BlockSpec((bb, L, dq_all), cur(1)),
            pl.BlockSpec((bb, L, dv_all), cur(1)),
            pl.BlockSpec((bb, L, dv_all), cur(2)),
            pl.BlockSpec((bb, L, LANES), cur(0)),
            pl.BlockSpec((1, LANES), lambda t: (0, 0)),
            pl.BlockSpec((1, dv_all), lambda t: (0, 0)),
            pl.BlockSpec((bb, L, D), prev),
            pl.BlockSpec((bb, L, kc), prev),
            pl.BlockSpec((kc + dv_all, D), lambda t: (0, 0), pipeline_mode=pl.Buffered(1)),
        ] + cast_in,
        out_specs=[pl.BlockSpec((bb, L, D), prev)] + cast_out,
        out_shape=[jax.ShapeDtypeStruct((batch, seq, D), _F32)] + cast_shapes,
        scratch_shapes=[
            pltpu.VMEM((bb * N_HEADS, dqk, dv), _F32),
            pltpu.VMEM((SUBLANES, dqk), _F32),
            pltpu.VMEM((SUBLANES, LANES), _F32),
            pltpu.VMEM((bb, L, dv_all), _BF16),
        ],
        compiler_params=pltpu.CompilerParams(
            dimension_semantics=("arbitrary",), vmem_limit_bytes=VMEM_LIMIT),
        name="mlstm_out",
    )(qkvo3, qkvo3, qkvo3, qkvo3, gates3, gate_bias, gnorm, x3, yc3, w_out,
      *[a for a, _, _ in cast_items])
    return (x1.reshape(batch * seq, D), *cast)


def _ffn_kernel(x_ref, g_ref, wg_ref, wu_ref, cwg_ref, cwu_ref, bg_ref, bu_ref, wd_ref,
                o_ref, h_ref, halog_ref, halou_ref, *, tiles_per_seq, rows):
    m = pl.program_id(0)
    j = pl.program_id(1)
    tm = x_ref.shape[0]

    @pl.when(j == 0)
    def _():
        x = x_ref[...]
        h_ref[...] = _rmsnorm_rows(x, g_ref[...]).astype(_BF16)
        o_ref[...] = x

    first = m % tiles_per_seq == 0
    prev_g = _load_halo(halog_ref, j, first)
    prev_u = _load_halo(halou_ref, j, first)

    def up_proj(r):
        h = h_ref[r * rows:(r + 1) * rows, :]
        return _dot(h, wg_ref[...]), _dot(h, wu_ref[...])

    n_chunks = tm // rows
    cur = up_proj(0)
    for r in range(n_chunks):
        nxt = up_proj(r + 1) if r + 1 < n_chunks else None
        zg, zu = cur
        gate = _causal_conv3(zg, prev_g, cwg_ref[...]) + bg_ref[...]
        up = _causal_conv3(zu, prev_u, cwu_ref[...]) + bu_ref[...]
        act = (jax.nn.silu(gate) * up).astype(_BF16)
        o_ref[r * rows:(r + 1) * rows, :] += _dot(act, wd_ref[...])
        prev_g = zg[rows - SUBLANES:, :]
        prev_u = zu[rows - SUBLANES:, :]
        cur = nxt
    halog_ref[j] = prev_g
    halou_ref[j] = prev_u


def _ffn(x2d, g, w_up, cw, cb, w_down, *, seq, tm=1024, tf=512, rows=512):
    M, D = x2d.shape
    F = w_down.shape[0]
    nj = F // tf
    return pl.pallas_call(
        functools.partial(_ffn_kernel, tiles_per_seq=seq // tm, rows=rows),
        grid=(M // tm, nj),
        in_specs=[
            pl.BlockSpec((tm, D), lambda m, j: (m, 0)),
            pl.BlockSpec((1, D), lambda m, j: (0, 0)),
            pl.BlockSpec((D, tf), lambda m, j: (0, j)),
            pl.BlockSpec((D, tf), lambda m, j: (0, nj + j)),
            pl.BlockSpec((CONV_WIDTH, tf), lambda m, j: (0, j)),
            pl.BlockSpec((CONV_WIDTH, tf), lambda m, j: (0, nj + j)),
            pl.BlockSpec((1, tf), lambda m, j: (0, j)),
            pl.BlockSpec((1, tf), lambda m, j: (0, nj + j)),
            pl.BlockSpec((tf, D), lambda m, j: (j, 0)),
        ],
        out_specs=pl.BlockSpec((tm, D), lambda m, j: (m, 0)),
        out_shape=jax.ShapeDtypeStruct((M, D), _F32),
        scratch_shapes=[
            pltpu.VMEM((tm, D), _BF16),
            pltpu.VMEM((nj, SUBLANES, tf), _F32),
            pltpu.VMEM((nj, SUBLANES, tf), _F32),
        ],
        compiler_params=pltpu.CompilerParams(
            dimension_semantics=("arbitrary", "arbitrary"), vmem_limit_bytes=VMEM_LIMIT),
        name="ffn",
    )(x2d, g, w_up, w_up, cw, cw, cb, cb, w_down)


def _ple_kernel(x_ref, g_ref, p_ref, wg_ref, wp_ref, gf_ref, o_ref, *, final_norm, rows):
    def normed(r):
        return _rmsnorm_rows(x_ref[r * rows:(r + 1) * rows, :], g_ref[...]).astype(_BF16)

    n_chunks = x_ref.shape[0] // rows
    h = normed(0)
    for r in range(n_chunks):
        h_next = normed(r + 1) if r + 1 < n_chunks else None
        sl = slice(r * rows, (r + 1) * rows)
        gate = jax.nn.sigmoid(_dot(h, wg_ref[...]))
        x = x_ref[sl, :] + gate * _dot(p_ref[sl, :].astype(_BF16), wp_ref[...])
        if final_norm:
            x = _rmsnorm_rows(x, gf_ref[...])
        o_ref[sl, :] = x
        h = h_next


def _ple(x2d, g, p2d, w_gate, w_proj, g_final, *, final_norm, tm=512, rows=256):
    M, D = x2d.shape
    P = p2d.shape[1]
    return pl.pallas_call(
        functools.partial(_ple_kernel, final_norm=final_norm, rows=rows),
        grid=(M // tm,),
        in_specs=[
            pl.BlockSpec((tm, D), lambda m: (m, 0)),
            pl.BlockSpec((1, D), lambda m: (0, 0)),
            pl.BlockSpec((tm, P), lambda m: (m, 0)),
            pl.BlockSpec((D, D), lambda m: (0, 0)),
            pl.BlockSpec((P, D), lambda m: (0, 0)),
            pl.BlockSpec((1, D), lambda m: (0, 0)),
        ],
        out_specs=pl.BlockSpec((tm, D), lambda m: (m, 0)),
        out_shape=jax.ShapeDtypeStruct((M, D), _F32),
        compiler_params=pltpu.CompilerParams(
            dimension_semantics=("arbitrary",), vmem_limit_bytes=VMEM_LIMIT),
        name="ple",
    )(x2d, g, p2d, w_gate, w_proj, g_final)


def kernel(x, p, norm_mix_g, w_in, b_igate, b_fgate, short_conv_w, mh_norm_g, w_out,
           norm_ffn_g, w_up, ffn_conv_w, ffn_conv_b, w_down, norm_ple_g, w_ple_gate,
           w_ple_proj, final_norm_g):
    B, S, D = x.shape
    depth = w_in.shape[0]
    d_conv = short_conv_w.shape[-1]
    d_mlstm = mh_norm_g.shape[-1]
    dv = d_mlstm // N_HEADS
    d_qk = (w_in.shape[-1] - 3 * d_conv - 2 * d_mlstm - 2 * N_HEADS) // 2
    dqk = d_qk // N_HEADS
    M = B * S

    x2d = x.reshape(M, D)
    row = lambda a: a.reshape(1, -1).astype(_F32)
    whole = lambda a: (a, 0, a.shape[0])
    colscale = jnp.concatenate([
        jnp.full((1, d_qk), dqk ** -0.5, _F32),
        jnp.ones((1, d_qk + 2 * d_mlstm), _F32)], axis=1)

    for i in range(depth):
        c0 = 3 * d_conv
        c1 = c0 + 2 * d_qk + 2 * d_mlstm
        w_t = jnp.swapaxes(w_in[i], 0, 1)
        w_gates = jnp.pad(w_t[c1:], ((0, LANES - 2 * N_HEADS), (0, 0))).astype(_BF16)
        gate_bias = jnp.pad(jnp.concatenate([b_igate[i], b_fgate[i]]).astype(_F32),
                            (0, LANES - 2 * N_HEADS)).reshape(1, LANES)

        qkvo, gates, h, w_conv_t = _qkvo(x2d, row(norm_mix_g[i]), w_t, c0, colscale, w_gates,
                                         [(w_t, 0, c0)])
        y_conv, w_up_b, w_out_b = _conv_group(h, w_conv_t, short_conv_w[i].astype(_F32),
                                              [whole(w_up[i]), whole(w_out[i])], seq=S)
        x2d, w_down_b, w_pg_b = _mlstm_out(
            qkvo, gates, gate_bias, row(mh_norm_g[i]), x2d, y_conv, w_out_b,
            [whole(w_down[i]), whole(w_ple_gate[i])], batch=B, seq=S, dqk=dqk, dv=dv)
        x2d = _ffn(x2d, row(norm_ffn_g[i]), w_up_b, ffn_conv_w[i].astype(_F32),
                   row(ffn_conv_b[i]), w_down_b, seq=S)
        x2d = _ple(x2d, row(norm_ple_g[i]), p[i].reshape(M, -1), w_pg_b,
                   w_ple_proj[i].astype(_BF16), row(final_norm_g),
                   final_norm=(i == depth - 1))
    return x2d.reshape(B, S, D)
```

```python
import functools

import jax
import jax.numpy as jnp
from jax import lax
from jax.experimental import pallas as pl
from jax.experimental.pallas import tpu as pltpu

N_HEADS = 4
CONV_WIDTH = 3
EPS = 1e-6
LANES = 128
SUBLANES = 8
MLSTM_CHUNK = 256
V7X_VMEM_BYTES = 64 * 1024 * 1024
VMEM_LIMIT = V7X_VMEM_BYTES - 2 * 1024 * 1024

_BF16 = jnp.bfloat16
_F32 = jnp.float32


def _dot(a, b):
    return jnp.dot(a, b, preferred_element_type=_F32)


def _dot_nt(a, bt):
    return lax.dot_general(a, bt, (((1,), (1,)), ((), ())), preferred_element_type=_F32)


def _rmsnorm_rows(x, g):
    ms = jnp.mean(x * x, axis=-1, keepdims=True)
    return x * lax.rsqrt(ms + EPS) * g


def _causal_conv3(z, prev, cw):
    n = z.shape[0]
    ext = jnp.concatenate([prev, z], axis=0)
    z2 = ext[SUBLANES - 2:SUBLANES - 2 + n, :]
    z1 = ext[SUBLANES - 1:SUBLANES - 1 + n, :]
    return z2 * cw[0:1, :] + z1 * cw[1:2, :] + z * cw[2:3, :]


def _load_halo(halo_ref, j, first_tile):
    @pl.when(first_tile)
    def _():
        halo_ref[j] = jnp.zeros(halo_ref.shape[1:], _F32)

    return halo_ref[j]


def _cast_plan(items, n_steps, step_index):
    in_specs, out_specs, shapes = [], [], []
    for a, row0, n_rows in items:
        slab = n_rows // n_steps
        assert slab * n_steps == n_rows and slab % (2 * SUBLANES) == 0 and row0 % slab == 0
        cols = a.shape[1]
        in_specs.append(pl.BlockSpec(
            (slab, cols), lambda *idx, b0=row0 // slab: (b0 + step_index(*idx), 0)))
        out_specs.append(pl.BlockSpec((slab, cols), lambda *idx: (step_index(*idx), 0)))
        shapes.append(jax.ShapeDtypeStruct((n_rows, cols), _BF16))
    return in_specs, out_specs, shapes


def _cast_slabs(refs):
    half = len(refs) // 2
    for src, dst in zip(refs[:half], refs[half:]):
        dst[...] = src[...].astype(dst.dtype)


def _qkvo_kernel(x_ref, g_ref, w_ref, cs_ref, wgate_ref, *rest, n_cast):
    cast_in = rest[:n_cast]
    o_ref, gate_ref, h_ref, *cast_out = rest[n_cast:2 * n_cast + 3]
    (wb_ref,) = rest[2 * n_cast + 3:]
    m = pl.program_id(0)
    j = pl.program_id(1)

    _cast_slabs(list(cast_in) + list(cast_out))

    @pl.when(j == 0)
    def _():
        h = _rmsnorm_rows(x_ref[...], g_ref[...]).astype(_BF16)
        h_ref[...] = h
        gate_ref[...] = _dot_nt(h, wgate_ref[...])

    @pl.when(m == 0)
    def _():
        wb_ref[j] = w_ref[...].astype(_BF16)

    o_ref[...] = (_dot_nt(h_ref[...], wb_ref[j]) * cs_ref[...]).astype(o_ref.dtype)


def _qkvo(x2d, g, w_in_t, row0, colscale, wgate, cast_items, *, tm=1024, tn=1024):
    M, D = x2d.shape
    N = colscale.shape[1]
    assert row0 % tn == 0 and N % tn == 0
    nj = N // tn
    cast_in, cast_out, cast_shapes = _cast_plan(
        cast_items, (M // tm) * nj, lambda m, j: m * nj + j)
    wtile = lambda m, j: (row0 // tn + jnp.where(m == 0, j, nj - 1), 0)
    return pl.pallas_call(
        functools.partial(_qkvo_kernel, n_cast=len(cast_items)),
        grid=(M // tm, nj),
        in_specs=[
            pl.BlockSpec((tm, D), lambda m, j: (m, 0)),
            pl.BlockSpec((1, D), lambda m, j: (0, 0)),
            pl.BlockSpec((tn, D), wtile, pipeline_mode=pl.Buffered(1)),
            pl.BlockSpec((1, tn), lambda m, j: (0, j)),
            pl.BlockSpec((LANES, D), lambda m, j: (0, 0)),
        ] + cast_in,
        out_specs=[
            pl.BlockSpec((tm, tn), lambda m, j: (m, j)),
            pl.BlockSpec((tm, LANES), lambda m, j: (m, 0)),
            pl.BlockSpec((tm, D), lambda m, j: (m, 0)),
        ] + cast_out,
        out_shape=[
            jax.ShapeDtypeStruct((M, N), _BF16),
            jax.ShapeDtypeStruct((M, LANES), _F32),
            jax.ShapeDtypeStruct((M, D), _BF16),
        ] + cast_shapes,
        scratch_shapes=[pltpu.VMEM((nj, tn, D), _BF16)],
        compiler_params=pltpu.CompilerParams(
            dimension_semantics=("arbitrary", "arbitrary"), vmem_limit_bytes=VMEM_LIMIT),
        name="qkvo",
    )(x2d, g, w_in_t, colscale, wgate, *[a for a, _, _ in cast_items])


def _conv_group_kernel(h_ref, w_ref, cw_ref, *rest, tiles_per_seq, rows, n_cast):
    cast_in, (o_ref, *cast_out) = rest[:n_cast], rest[n_cast:2 * n_cast + 1]
    (halo_ref,) = rest[2 * n_cast + 1:]
    m = pl.program_id(0)
    j = pl.program_id(1)
    tm, tc = o_ref.shape
    n_ch = w_ref.shape[0] // 3

    _cast_slabs(list(cast_in) + list(cast_out))

    def wtile(part):
        return w_ref[pl.ds(pl.multiple_of(part * n_ch + j * tc, tc), tc), :]

    def proj(r):
        h = h_ref[r * rows:(r + 1) * rows, :]
        return _dot_nt(h, wtile(0)), _dot_nt(h, wtile(1)), _dot_nt(h, wtile(2))

    prev = _load_halo(halo_ref, j, m % tiles_per_seq == 0)
    n_chunks = tm // rows
    cur = proj(0)
    for r in range(n_chunks):
        nxt = proj(r + 1) if r + 1 < n_chunks else None
        gb, gc, u = cur
        z = gc * u
        y = _causal_conv3(z, prev, cw_ref[...])
        o_ref[r * rows:(r + 1) * rows, :] = (gb * y).astype(o_ref.dtype)
        prev = z[rows - SUBLANES:, :]
        cur = nxt
    halo_ref[j] = prev


def _conv_group(h2d, w_t, cw, cast_items, *, seq, tm=1024, tc=512, rows=256):
    M, D = h2d.shape
    C = cw.shape[1]
    assert w_t.shape == (3 * C, D) and C % tc == 0
    nj = C // tc
    cast_in, cast_out, cast_shapes = _cast_plan(
        cast_items, (M // tm) * nj, lambda m, j: m * nj + j)
    return pl.pallas_call(
        functools.partial(_conv_group_kernel, tiles_per_seq=seq // tm, rows=rows,
                          n_cast=len(cast_items)),
        grid=(M // tm, nj),
        in_specs=[
            pl.BlockSpec((tm, D), lambda m, j: (m, 0)),
            pl.BlockSpec((3 * C, D), lambda m, j: (0, 0), pipeline_mode=pl.Buffered(1)),
            pl.BlockSpec((CONV_WIDTH, tc), lambda m, j: (0, j)),
        ] + cast_in,
        out_specs=[pl.BlockSpec((tm, tc), lambda m, j: (m, j))] + cast_out,
        out_shape=[jax.ShapeDtypeStruct((M, C), _BF16)] + cast_shapes,
        scratch_shapes=[pltpu.VMEM((nj, SUBLANES, tc), _F32)],
        compiler_params=pltpu.CompilerParams(
            dimension_semantics=("arbitrary", "arbitrary"), vmem_limit_bytes=VMEM_LIMIT),
        name="conv_group",
    )(h2d, w_t, cw, *[a for a, _, _ in cast_items])


def _exact_cumsum_rows(tril, a):
    hi = a.astype(_BF16)
    r1 = a - hi.astype(_F32)
    mid = r1.astype(_BF16)
    lo = (r1 - mid.astype(_F32)).astype(_BF16)
    return _dot(tril, hi) + _dot(tril, mid) + _dot(tril, lo)


def _mlstm_out_kernel(q_ref, k_ref, v_ref, o_ref, gate_ref, bias_ref, gn_ref,
                      x_ref, yc_ref, wo_ref, *rest, dqk, dv, n_cast, nc):
    cast_in, (x1_ref, *cast_out) = rest[:n_cast], rest[n_cast:2 * n_cast + 1]
    c_ref, n_ref, m_ref, ym_ref = rest[2 * n_cast + 1:]
    t = pl.program_id(0)
    bb, L = q_ref.shape[0], q_ref.shape[1]
    kc = yc_ref.shape[2]

    _cast_slabs(list(cast_in) + list(cast_out))

    @pl.when(t % nc == 0)
    def _():
        c_ref[...] = jnp.zeros(c_ref.shape, _F32)
        n_ref[...] = jnp.zeros(n_ref.shape, _F32)
        m_ref[...] = jnp.zeros(m_ref.shape, _F32)

    @pl.when(t == 0)
    def _():
        ym_ref[...] = jnp.zeros(ym_ref.shape, ym_ref.dtype)

    def out_proj(bi):
        acc = _dot(yc_ref[bi], wo_ref[0:kc, :]) + _dot(ym_ref[bi], wo_ref[kc:, :])
        x1_ref[bi] = x_ref[bi] + acc

    row = lax.broadcasted_iota(jnp.int32, (L, L), 0)
    col = lax.broadcasted_iota(jnp.int32, (L, L), 1)
    causal = col <= row
    tril = causal.astype(_BF16)

    chains = [(bi, h) for bi in range(bb) for h in range(N_HEADS)]
    pre = {}
    for bi in range(bb):
        gates = gate_ref[bi] + bias_ref[...]
        logf = jnp.minimum(gates, 0.0) - jnp.log1p(jnp.exp(-jnp.abs(gates)))
        bcum = _exact_cumsum_rows(tril, logf)
        pre[bi] = (gates, bcum, gates.T, bcum.T)

    stab = {}
    for bi, h in chains:
        gates, bcum, gates_t, bcum_t = pre[bi]
        st = bi * N_HEADS + h
        b_col = bcum[:, N_HEADS + h:N_HEADS + h + 1]
        b_row = bcum_t[N_HEADS + h:N_HEADS + h + 1, :]
        dlog = jnp.where(causal, (b_col - b_row) + gates_t[h:h + 1, :], -jnp.inf)
        inter = b_col + m_ref[st:st + 1, 0:1]
        m_t = jnp.maximum(inter, jnp.max(dlog, axis=-1, keepdims=True))
        stab[bi, h] = (b_col, dlog, inter, m_t)

    for bi in range(0, bb, 2):
        out_proj(bi)

    out = {}
    for bi, h in chains:
        st = bi * N_HEADS + h
        b_col, dlog, inter, m_t = stab[bi, h]
        dw = jnp.exp(dlog - m_t)
        a_inter = jnp.exp(inter - m_t)
        q = q_ref[bi, :, h * dqk:(h + 1) * dqk]
        k = k_ref[bi, :, h * dqk:(h + 1) * dqk]
        v = v_ref[bi, :, h * dv:(h + 1) * dv]
        s = _dot_nt(q, k) * dw
        num = a_inter * _dot(q, c_ref[st].astype(_BF16)) + _dot(s.astype(_BF16), v)
        qn = jnp.sum(q.astype(_F32) * n_ref[st:st + 1, :], axis=-1, keepdims=True)
        den = a_inter * qn + jnp.sum(s, axis=-1, keepdims=True)
        den = jnp.maximum(jnp.abs(den), jnp.exp(-m_t))
        out[bi, h] = (num / den, a_inter)

    for bi in range(1, bb, 2):
        out_proj(bi)

    for bi, h in chains:
        hm, _ = out[bi, h]
        hm = hm * lax.rsqrt(jnp.mean(hm * hm, axis=-1, keepdims=True) + EPS)
        hm = hm * gn_ref[:, h * dv:(h + 1) * dv]
        og = jax.nn.sigmoid(o_ref[bi, :, h * dv:(h + 1) * dv].astype(_F32))
        ym_ref[bi, :, h * dv:(h + 1) * dv] = (og * hm).astype(ym_ref.dtype)

    for bi, h in chains:
        st = bi * N_HEADS + h
        gates = pre[bi][0]
        b_col, _, _, m_t = stab[bi, h]
        a_inter = out[bi, h][1]
        k = k_ref[bi, :, h * dqk:(h + 1) * dqk]
        v = v_ref[bi, :, h * dv:(h + 1) * dv]
        b_tot = b_col[L - 1:L, :]
        m_new = m_t[L - 1:L, :]
        a_prev = a_inter[L - 1:L, :]
        w_col = jnp.exp(b_tot - b_col + gates[:, h:h + 1] - m_new)
        kw = k.astype(_F32) * w_col
        c_ref[st] = a_prev * c_ref[st] + _dot(kw.T.astype(_BF16), v)
        n_ref[st:st + 1, :] = a_prev * n_ref[st:st + 1, :] + jnp.sum(kw, axis=0, keepdims=True)
        m_ref[st:st + 1, :] = jnp.broadcast_to(m_new, (1, LANES))


def _mlstm_out(qkvo, gates, gate_bias, gnorm, x2d, y_conv, w_out, cast_items, *,
               batch, seq, dqk, dv, bb=2):
    L = MLSTM_CHUNK
    nc = seq // L
    D = x2d.shape[1]
    dq_all = N_HEADS * dqk
    dv_all = N_HEADS * dv
    kc = y_conv.shape[1]
    assert dv_all == 2 * dq_all and batch % bb == 0 and bb * N_HEADS <= SUBLANES
    assert w_out.shape == (kc + dv_all, D)
    n_blocks = (batch // bb) * nc
    qkvo3 = qkvo.reshape(batch, seq, -1)
    gates3 = gates.reshape(batch, seq, LANES)
    x3 = x2d.reshape(batch, seq, D)
    yc3 = y_conv.reshape(batch, seq, kc)

    def cur(col):
        def index(t):
            tb = jnp.minimum(t, n_blocks - 1)
            return tb // nc, tb % nc, col
        return index

    def prev(t):
        tb = jnp.maximum(t - 1, 0)
        return tb // nc, tb % nc, 0

    cast_in, cast_out, cast_shapes = _cast_plan(
        cast_items, n_blocks, lambda t: jnp.minimum(t, n_blocks - 1))
    x1, *cast = pl.pallas_call(
        functools.partial(_mlstm_out_kernel, dqk=dqk, dv=dv, n_cast=len(cast_items), nc=nc),
        grid=(n_blocks + 1,),
        in_specs=[
            pl.BlockSpec((bb, L, dq_all), cur(0)),
            pl.BlockSpec((bb, L, dq_all), cur(1)),
            pl.BlockSpec((bb, L, dv_all), cur(1)),
            pl.BlockSpec((bb, L, dv_all), cur(2)),
            pl.BlockSpec((bb, L, LANES), cur(0)),
            pl.BlockSpec((1, LANES), lambda t: (0, 0)),
            pl.BlockSpec((1, dv_all), lambda t: (0, 0)),
            pl.BlockSpec((bb, L, D), prev),
            pl.BlockSpec((bb, L, kc), prev),
            pl.BlockSpec((kc + dv_all, D), lambda t: (0, 0), pipeline_mode=pl.Buffered(1)),
        ] + cast_in,
        out_specs=[pl.BlockSpec((bb, L, D), prev)] + cast_out,
        out_shape=[jax.ShapeDtypeStruct((batch, seq, D), _F32)] + cast_shapes,
        scratch_shapes=[
            pltpu.VMEM((bb * N_HEADS, dqk, dv), _F32),
            pltpu.VMEM((SUBLANES, dqk), _F32),
            pltpu.VMEM((SUBLANES, LANES), _F32),
            pltpu.VMEM((bb, L, dv_all), _BF16),
        ],
        compiler_params=pltpu.CompilerParams(
            dimension_semantics=("arbitrary",), vmem_limit_bytes=VMEM_LIMIT),
        name="mlstm_out",
    )(qkvo3, qkvo3, qkvo3, qkvo3, gates3, gate_bias, gnorm, x3, yc3, w_out,
      *[a for a, _, _ in cast_items])
    return (x1.reshape(batch * seq, D), *cast)


def _ffn_kernel(x_ref, g_ref, wg_ref, wu_ref, cwg_ref, cwu_ref, bg_ref, bu_ref, wd_ref,
                o_ref, h_ref, halog_ref, halou_ref, *, tiles_per_seq, rows):
    m = pl.program_id(0)
    j = pl.program_id(1)
    tm = x_ref.shape[0]

    @pl.when(j == 0)
    def _():
        x = x_ref[...]
        h_ref[...] = _rmsnorm_rows(x, g_ref[...]).astype(_BF16)
        o_ref[...] = x

    first = m % tiles_per_seq == 0
    prev_g = _load_halo(halog_ref, j, first)
    prev_u = _load_halo(halou_ref, j, first)

    def up_proj(r):
        h = h_ref[r * rows:(r + 1) * rows, :]
        return _dot(h, wg_ref[...]), _dot(h, wu_ref[...])

    n_chunks = tm // rows
    cur = up_proj(0)
    for r in range(n_chunks):
        nxt = up_proj(r + 1) if r + 1 < n_chunks else None
        zg, zu = cur
        gate = _causal_conv3(zg, prev_g, cwg_ref[...]) + bg_ref[...]
        up = _causal_conv3(zu, prev_u, cwu_ref[...]) + bu_ref[...]
        act = (jax.nn.silu(gate) * up).astype(_BF16)
        o_ref[r * rows:(r + 1) * rows, :] += _dot(act, wd_ref[...])
        prev_g = zg[rows - SUBLANES:, :]
        prev_u = zu[rows - SUBLANES:, :]
        cur = nxt
    halog_ref[j] = prev_g
    halou_ref[j] = prev_u


def _ffn(x2d, g, w_up, cw, cb, w_down, *, seq, tm=1024, tf=512, rows=512):
    M, D = x2d.shape
    F = w_down.shape[0]
    nj = F // tf
    return pl.pallas_call(
        functools.partial(_ffn_kernel, tiles_per_seq=seq // tm, rows=rows),
        grid=(M // tm, nj),
        in_specs=[
            pl.BlockSpec((tm, D), lambda m, j: (m, 0)),
            pl.BlockSpec((1, D), lambda m, j: (0, 0)),
            pl.BlockSpec((D, tf), lambda m, j: (0, j)),
            pl.BlockSpec((D, tf), lambda m, j: (0, nj + j)),
            pl.BlockSpec((CONV_WIDTH, tf), lambda m, j: (0, j)),
            pl.BlockSpec((CONV_WIDTH, tf), lambda m, j: (0, nj + j)),
            pl.BlockSpec((1, tf), lambda m, j: (0, j)),
            pl.BlockSpec((1, tf), lambda m, j: (0, nj + j)),
            pl.BlockSpec((tf, D), lambda m, j: (j, 0)),
        ],
        out_specs=pl.BlockSpec((tm, D), lambda m, j: (m, 0)),
        out_shape=jax.ShapeDtypeStruct((M, D), _F32),
        scratch_shapes=[
            pltpu.VMEM((tm, D), _BF16),
            pltpu.VMEM((nj, SUBLANES, tf), _F32),
            pltpu.VMEM((nj, SUBLANES, tf), _F32),
        ],
        compiler_params=pltpu.CompilerParams(
            dimension_semantics=("arbitrary", "arbitrary"), vmem_limit_bytes=VMEM_LIMIT),
        name="ffn",
    )(x2d, g, w_up, w_up, cw, cw, cb, cb, w_down)


def _ple_kernel(x_ref, g_ref, p_ref, wg_ref, wp_ref, gf_ref, o_ref, *, final_norm, rows):
    def normed(r):
        return _rmsnorm_rows(x_ref[r * rows:(r + 1) * rows, :], g_ref[...]).astype(_BF16)

    n_chunks = x_ref.shape[0] // rows
    h = normed(0)
    for r in range(n_chunks):
        h_next = normed(r + 1) if r + 1 < n_chunks else None
        sl = slice(r * rows, (r + 1) * rows)
        gate = jax.nn.sigmoid(_dot(h, wg_ref[...]))
        x = x_ref[sl, :] + gate * _dot(p_ref[sl, :].astype(_BF16), wp_ref[...])
        if final_norm:
            x = _rmsnorm_rows(x, gf_ref[...])
        o_ref[sl, :] = x
        h = h_next


def _ple(x2d, g, p2d, w_gate, w_proj, g_final, *, final_norm, tm=512, rows=256):
    M, D = x2d.shape
    P = p2d.shape[1]
    return pl.pallas_call(
        functools.partial(_ple_kernel, final_norm=final_norm, rows=rows),
        grid=(M // tm,),
        in_specs=[
            pl.BlockSpec((tm, D), lambda m: (m, 0)),
            pl.BlockSpec((1, D), lambda m: (0, 0)),
            pl.BlockSpec((tm, P), lambda m: (m, 0)),
            pl.BlockSpec((D, D), lambda m: (0, 0)),
            pl.BlockSpec((P, D), lambda m: (0, 0)),
            pl.BlockSpec((1, D), lambda m: (0, 0)),
        ],
        out_specs=pl.BlockSpec((tm, D), lambda m: (m, 0)),
        out_shape=jax.ShapeDtypeStruct((M, D), _F32),
        compiler_params=pltpu.CompilerParams(
            dimension_semantics=("arbitrary",), vmem_limit_bytes=VMEM_LIMIT),
        name="ple",
    )(x2d, g, p2d, w_gate, w_proj, g_final)


def kernel(x, p, norm_mix_g, w_in, b_igate, b_fgate, short_conv_w, mh_norm_g, w_out,
           norm_ffn_g, w_up, ffn_conv_w, ffn_conv_b, w_down, norm_ple_g, w_ple_gate,
           w_ple_proj, final_norm_g):
    B, S, D = x.shape
    depth = w_in.shape[0]
    d_conv = short_conv_w.shape[-1]
    d_mlstm = mh_norm_g.shape[-1]
    dv = d_mlstm // N_HEADS
    d_qk = (w_in.shape[-1] - 3 * d_conv - 2 * d_mlstm - 2 * N_HEADS) // 2
    dqk = d_qk // N_HEADS
    M = B * S

    x2d = x.reshape(M, D)
    row = lambda a: a.reshape(1, -1).astype(_F32)
    whole = lambda a: (a, 0, a.shape[0])
    colscale = jnp.concatenate([
        jnp.full((1, d_qk), dqk ** -0.5, _F32),
        jnp.ones((1, d_qk + 2 * d_mlstm), _F32)], axis=1)

    for i in range(depth):
        c0 = 3 * d_conv
        c1 = c0 + 2 * d_qk + 2 * d_mlstm
        w_t = jnp.swapaxes(w_in[i], 0, 1)
        w_gates = jnp.pad(w_t[c1:], ((0, LANES - 2 * N_HEADS), (0, 0))).astype(_BF16)
        gate_bias = jnp.pad(jnp.concatenate([b_igate[i], b_fgate[i]]).astype(_F32),
                            (0, LANES - 2 * N_HEADS)).reshape(1, LANES)

        qkvo, gates, h, w_conv_t = _qkvo(x2d, row(norm_mix_g[i]), w_t, c0, colscale, w_gates,
                                         [(w_t, 0, c0)])
        y_conv, w_up_b, w_out_b = _conv_group(h, w_conv_t, short_conv_w[i].astype(_F32),
                                              [whole(w_up[i]), whole(w_out[i])], seq=S)
        x2d, w_down_b, w_pg_b = _mlstm_out(
            qkvo, gates, gate_bias, row(mh_norm_g[i]), x2d, y_conv, w_out_b,
            [whole(w_down[i]), whole(w_ple_gate[i])], batch=B, seq=S, dqk=dqk, dv=dv)
        x2d = _ffn(x2d, row(norm_ffn_g[i]), w_up_b, ffn_conv_w[i].astype(_F32),
                   row(ffn_conv_b[i]), w_down_b, seq=S)
        x2d = _ple(x2d, row(norm_ple_g[i]), p[i].reshape(M, -1), w_pg_b,
                   w_ple_proj[i].astype(_BF16), row(final_norm_g),
                   final_norm=(i == depth - 1))
    return x2d.reshape(B, S, D)
```
